```python
import math
import jax
import jax.numpy as jnp
from jax import lax
import numpy as np

D_MODEL = 1024
BATCH = 4
SEQ = 4096
DEPTH = 1
DEC_BATCH = 8
DEC_SEQ = 32
PAST_LEN = 4096

CHUNK = 64
D_MIX = D_MODEL
D_S5 = D_MIX // 2
S5_GROUP = 16
S5_GROUPS = D_S5 // S5_GROUP
S5_STATE = 64
D_FOX = D_MIX - D_S5
FOX_HEAD_DIM = 64
FOX_HEADS = D_FOX // FOX_HEAD_DIM
D_IN = D_S5 + 3 * D_FOX + FOX_HEADS
SPLITS = (D_S5, D_S5 + D_FOX, D_S5 + 2 * D_FOX, D_S5 + 3 * D_FOX)
D_FF = 2816
N_MOD = 9
Q_BLOCK = 128
ATTN_SCALE = FOX_HEAD_DIM ** -0.5
NEG_INF = -1e30
EPS = 1e-6
FORGET_BIAS_INIT = 4.0
DT_MIN = 0.001
DT_MAX = 0.1

kernel_name = 'hymba_s5_fox_macaron_adaln_stream_step'


def rmsnorm(x, g):
    x32 = x.astype(jnp.float32)
    y = x32 * lax.rsqrt(jnp.mean(x32 * x32, axis=-1, keepdims=True) + EPS)
    return (y * g.astype(jnp.float32)).astype(x.dtype)


def modulate(h, shift, scale):
    return h * (1 + scale) + shift


def swiglu(h, w_up, w_down):
    gu = jnp.einsum('bld,df->blf', h, w_up)
    g, u = jnp.split(gu, 2, axis=-1)
    return jnp.einsum('blf,fd->bld', jax.nn.silu(g) * u, w_down)


def _ssm_combine(e1, e2):
    a1, b1 = e1
    a2, b2 = e2
    return a1 * a2, a2 * b1 + b2


def s5_branch(u, h0, p):
    f32 = jnp.float32
    bsz, L, _ = u.shape
    lam = lax.complex(p['lam_re'].astype(f32), p['lam_im'].astype(f32))
    dt = jnp.exp(p['log_dt'].astype(f32))[:, None]
    a_bar = jnp.exp(lam * dt)
    b_mat = lax.complex(p['s5_b_re'].astype(f32), p['s5_b_im'].astype(f32))
    b_bar = ((a_bar - 1.0) / lam)[..., None] * b_mat
    c_mat = lax.complex(p['s5_c_re'].astype(f32), p['s5_c_im'].astype(f32))
    u32 = u.astype(f32)
    bu = jnp.einsum('blgc,gpc->blgp', u32.reshape(bsz, L, S5_GROUPS, S5_GROUP), b_bar)
    a_seq = jnp.broadcast_to(a_bar, bu.shape)
    a_cum, states = lax.associative_scan(_ssm_combine, (a_seq, bu), axis=1)
    states = states + a_cum * h0[:, None]
    y = jnp.real(jnp.einsum('blgp,gcp->blgc', states, c_mat)).reshape(bsz, L, D_S5)
    y = y + p['s5_d'].astype(f32) * u32
    y = jax.nn.gelu(y)
    y = y * jax.nn.sigmoid(jnp.einsum('ble,ef->blf', y, p['w_glu'].astype(f32)) + p['b_glu'].astype(f32))
    last = states[:, -1]
    return y.astype(u.dtype), jnp.real(last), jnp.imag(last)


def fox_attention(q, k, v, Fq, Fk, q_pos, k_pos):
    f32 = jnp.float32
    s = jnp.einsum('bqhd,bkhd->bhqk', q.astype(f32), k.astype(f32)) * ATTN_SCALE
    s = s + jnp.transpose(Fq, (0, 2, 1))[..., :, None] - jnp.transpose(Fk, (0, 2, 1))[..., None, :]
    mask = k_pos[None, :] <= q_pos[:, None]
    s = jnp.where(mask, s, NEG_INF)
    pr = jax.nn.softmax(s, axis=-1)
    return jnp.einsum('bhqk,bkhd->bqhd', pr, v.astype(f32))


def hybrid_mixer(h, p, cache):
    f32 = jnp.float32
    bsz, L, _ = h.shape
    proj = jnp.einsum('bld,de->ble', h, p['w_in'])
    u, q, k, v, fl = jnp.split(proj, SPLITS, axis=-1)
    q = q.reshape(bsz, L, FOX_HEADS, FOX_HEAD_DIM)
    k = k.reshape(bsz, L, FOX_HEADS, FOX_HEAD_DIM)
    v = v.reshape(bsz, L, FOX_HEADS, FOX_HEAD_DIM)
    logf = jax.nn.log_sigmoid(fl.astype(f32) + p['b_fgate'].astype(f32))
    if cache is None:
        h0 = jnp.zeros((bsz, S5_GROUPS, S5_STATE), jnp.complex64)
        F = jnp.cumsum(logf, axis=1)
        pos = jnp.arange(L)
        nb = L // Q_BLOCK

        def attend(blk):
            qb, Fqb, pb = blk
            return fox_attention(qb, k, v, Fqb, F, pb, pos)

        o = lax.map(attend, (q.reshape(bsz, nb, Q_BLOCK, FOX_HEADS, FOX_HEAD_DIM).swapaxes(0, 1),
                             F.reshape(bsz, nb, Q_BLOCK, FOX_HEADS).swapaxes(0, 1),
                             pos.reshape(nb, Q_BLOCK)))
        o = o.swapaxes(0, 1).reshape(bsz, L, D_FOX)
    else:
        cache_k, cache_v, cache_logf, s_re, s_im = cache
        past = cache_k.shape[1]
        h0 = lax.complex(s_re.astype(f32), s_im.astype(f32))
        k_all = jnp.concatenate([cache_k.astype(k.dtype), k], axis=1)
        v_all = jnp.concatenate([cache_v.astype(v.dtype), v], axis=1)
        F_all = jnp.cumsum(jnp.concatenate([cache_logf.astype(f32), logf], axis=1), axis=1)
        o = fox_attention(q, k_all, v_all, F_all[:, past:], F_all,
                          past + jnp.arange(L), jnp.arange(past + L))
        o = o.reshape(bsz, L, D_FOX)
    y_s5, s_re_new, s_im_new = s5_branch(u, h0, p)
    mixed = jnp.concatenate([rmsnorm(y_s5, p['g_s5_out']),
                             rmsnorm(o.astype(h.dtype), p['g_fox_out'])], axis=-1)
    out = jnp.einsum('ble,ed->bld', mixed, p['w_out'])
    return out, (k, v, logf, s_re_new, s_im_new)


def trunk_layer(x, c, p, cache):
    mod = jnp.einsum('bd,de->be', jax.nn.silu(c), p['w_ada']) + p['b_ada']
    sh1, sc1, g1, sh2, sc2, g2, sh3, sc3, g3 = [m[:, None, :] for m in jnp.split(mod, N_MOD, axis=-1)]
    h = modulate(rmsnorm(x, p['g_ffn1']), sh1, sc1)
    x = x + 0.5 * (1 + g1) * swiglu(h, p['w_up1'], p['w_down1'])
    h = modulate(rmsnorm(x, p['g_mix']), sh2, sc2)
    y, new_state = hybrid_mixer(h, p, cache)
    x = x + (1 + g2) * y
    h = modulate(rmsnorm(x, p['g_ffn2']), sh3, sc3)
    x = x + 0.5 * (1 + g3) * swiglu(h, p['w_up2'], p['w_down2'])
    return x, new_state


def setup_inputs(seed: int = 0) -> dict:
    key = jax.random.key(seed)
    ks = jax.random.split(key, 36)
    f32 = jnp.float32

    def nrm(k, shape, scale=1.0):
        return jax.random.normal(k, shape, f32) * scale

    n = jnp.arange(S5_STATE, dtype=f32)
    return {
        'x_prompt': nrm(ks[0], (BATCH, SEQ, D_MODEL)),
        'x_sample': nrm(ks[1], (DEC_BATCH, DEC_SEQ, D_MODEL)),
        'c_prompt': nrm(ks[2], (BATCH, D_MODEL)),
        'c_sample': nrm(ks[3], (DEC_BATCH, D_MODEL)),
        'cache_fox_k': nrm(ks[4], (DEPTH, DEC_BATCH, PAST_LEN, FOX_HEADS, FOX_HEAD_DIM)),
        'cache_fox_v': nrm(ks[5], (DEPTH, DEC_BATCH, PAST_LEN, FOX_HEADS, FOX_HEAD_DIM)),
        'cache_fox_logf': jax.nn.log_sigmoid(FORGET_BIAS_INIT + nrm(ks[6], (DEPTH, DEC_BATCH, PAST_LEN, FOX_HEADS))),
        'state_s5_re': nrm(ks[7], (DEPTH, DEC_BATCH, S5_GROUPS, S5_STATE), 0.1),
        'state_s5_im': nrm(ks[8], (DEPTH, DEC_BATCH, S5_GROUPS, S5_STATE), 0.1),
        'w_ada': nrm(ks[9], (DEPTH, D_MODEL, N_MOD * D_MODEL), 0.1 * D_MODEL ** -0.5),
        'b_ada': nrm(ks[10], (DEPTH, N_MOD * D_MODEL), 0.01),
        'g_ffn1': 1.0 + nrm(ks[11], (DEPTH, D_MODEL), 0.01),
        'w_up1': nrm(ks[12], (DEPTH, D_MODEL, 2 * D_FF), D_MODEL ** -0.5),
        'w_down1': nrm(ks[13], (DEPTH, D_FF, D_MODEL), D_FF ** -0.5),
        'g_mix': 1.0 + nrm(ks[14], (DEPTH, D_MODEL), 0.01),
        'w_in': nrm(ks[15], (DEPTH, D_MODEL, D_IN), D_MODEL ** -0.5),
        'b_fgate': FORGET_BIAS_INIT + nrm(ks[16], (DEPTH, FOX_HEADS), 0.1),
        'lam_re': -0.5 + nrm(ks[17], (DEPTH, S5_GROUPS, S5_STATE), 0.01),
        'lam_im': math.pi * n + nrm(ks[18], (DEPTH, S5_GROUPS, S5_STATE), 0.01),
        'log_dt': jax.random.uniform(ks[19], (DEPTH, S5_GROUPS), f32, math.log(DT_MIN), math.log(DT_MAX)),
        's5_b_re': nrm(ks[20], (DEPTH, S5_GROUPS, S5_STATE, S5_GROUP), (2 * S5_GROUP) ** -0.5),
        's5_b_im': nrm(ks[21], (DEPTH, S5_GROUPS, S5_STATE, S5_GROUP), (2 * S5_GROUP) ** -0.5),
        's5_c_re': nrm(ks[22], (DEPTH, S5_GROUPS, S5_GROUP, S5_STATE), (2 * S5_STATE) ** -0.5),
        's5_c_im': nrm(ks[23], (DEPTH, S5_GROUPS, S5_GROUP, S5_STATE), (2 * S5_STATE) ** -0.5),
        's5_d': nrm(ks[24], (DEPTH, D_S5), 0.3),
        'w_glu': nrm(ks[25], (DEPTH, D_S5, D_S5), D_S5 ** -0.5),
        'b_glu': nrm(ks[26], (DEPTH, D_S5), 0.01),
        'g_s5_out': 1.0 + nrm(ks[27], (DEPTH, D_S5), 0.01),
        'g_fox_out': 1.0 + nrm(ks[28], (DEPTH, D_FOX), 0.01),
        'w_out': nrm(ks[29], (DEPTH, D_MIX, D_MODEL), D_MIX ** -0.5),
        'g_ffn2': 1.0 + nrm(ks[30], (DEPTH, D_MODEL), 0.01),
        'w_up2': nrm(ks[31], (DEPTH, D_MODEL, 2 * D_FF), D_MODEL ** -0.5),
        'w_down2': nrm(ks[32], (DEPTH, D_FF, D_MODEL), D_FF ** -0.5),
        'g_final': 1.0 + nrm(ks[33], (D_MODEL,), 0.01),
    }


def reference(x_prompt, x_sample, c_prompt, c_sample, cache_fox_k, cache_fox_v, cache_fox_logf,
              state_s5_re, state_s5_im, w_ada, b_ada, g_ffn1, w_up1, w_down1, g_mix, w_in, b_fgate,
              lam_re, lam_im, log_dt, s5_b_re, s5_b_im, s5_c_re, s5_c_im, s5_d, w_glu, b_glu,
              g_s5_out, g_fox_out, w_out, g_ffn2, w_up2, w_down2, g_final):
    xp, xs = x_prompt, x_sample
    new_p, new_s = [], []
    for l in range(DEPTH):
        p = dict(w_ada=w_ada[l], b_ada=b_ada[l], g_ffn1=g_ffn1[l], w_up1=w_up1[l], w_down1=w_down1[l],
                 g_mix=g_mix[l], w_in=w_in[l], b_fgate=b_fgate[l], lam_re=lam_re[l], lam_im=lam_im[l],
                 log_dt=log_dt[l], s5_b_re=s5_b_re[l], s5_b_im=s5_b_im[l], s5_c_re=s5_c_re[l],
                 s5_c_im=s5_c_im[l], s5_d=s5_d[l], w_glu=w_glu[l], b_glu=b_glu[l],
                 g_s5_out=g_s5_out[l], g_fox_out=g_fox_out[l], w_out=w_out[l], g_ffn2=g_ffn2[l],
                 w_up2=w_up2[l], w_down2=w_down2[l])
        xp, st_p = trunk_layer(xp, c_prompt, p, None)
        xs, st_s = trunk_layer(xs, c_sample, p, (cache_fox_k[l], cache_fox_v[l], cache_fox_logf[l],
                                                 state_s5_re[l], state_s5_im[l]))
        new_p.append(st_p)
        new_s.append(st_s)
    y_prompt = rmsnorm(xp, g_final)
    y_sample = rmsnorm(xs, g_final)
    k_p = jnp.stack([s[0] for s in new_p])
    v_p = jnp.stack([s[1] for s in new_p])
    logf_p = jnp.stack([s[2] for s in new_p])
    s5re_p = jnp.stack([s[3] for s in new_p])
    s5im_p = jnp.stack([s[4] for s in new_p])
    k_s = jnp.stack([s[0] for s in new_s])
    v_s = jnp.stack([s[1] for s in new_s])
    logf_s = jnp.stack([s[2] for s in new_s])
    s5re_s = jnp.stack([s[3] for s in new_s])
    s5im_s = jnp.stack([s[4] for s in new_s])
    return (y_prompt, y_sample, k_p, v_p, logf_p, s5re_p, s5im_p, k_s, v_s, logf_s, s5re_s, s5im_s)
```

```python
import functools
import math

import jax
import jax.numpy as jnp
from jax import lax
from jax.experimental import pallas as pl
from jax.experimental.pallas import tpu as pltpu

F32 = jnp.float32
BF16 = jnp.bfloat16

D_MODEL = 1024
D_FF = 2816
D_S5 = 512
D_FOX = 512
FOX_HEADS = 8
FOX_HEAD_DIM = 64
S5_GROUPS = 32
S5_GROUP = 16
S5_STATE = 64
N_STATE = S5_GROUPS * S5_STATE
N_MOD = 9
ATTN_SCALE = FOX_HEAD_DIM ** -0.5
NEG_INF = -1e30
EPS = 1e-6
LANES = 128
SUBLANES = 8
VMEM_LIMIT = 56 * 1024 * 1024


def _params(n_axes=1, vmem=VMEM_LIMIT):
    return pltpu.CompilerParams(dimension_semantics=("arbitrary",) * n_axes, vmem_limit_bytes=vmem)


def _resident(shape):
    nd = len(shape)
    return pl.BlockSpec(shape, lambda *_: (0,) * nd, pipeline_mode=pl.Buffered(1))


def _sigmoid(x):
    return 1.0 / (1.0 + jnp.exp(-x))


def _rms(x, g):
    y = x * lax.rsqrt(jnp.mean(x * x, axis=-1, keepdims=True) + EPS)
    return y * g


def _dot(a, b):
    return jnp.dot(a, b, preferred_element_type=F32)


def _dot_nt(a, b):
    return lax.dot_general(a, b, (((1,), (1,)), ((), ())), preferred_element_type=F32)


def _split3(x):
    hi = x.astype(BF16)
    r = x - hi.astype(F32)
    mid = r.astype(BF16)
    lo = (r - mid.astype(F32)).astype(BF16)
    return hi, mid, lo


def _ada_kernel(c_ref, w_ref, b_ref, o_ref):
    c = c_ref[...]
    s = (c * _sigmoid(c)).astype(BF16)
    o_ref[...] = _dot(s, w_ref[...].astype(BF16)) + b_ref[...]


def _ada_call(c_all, w_ada, b_ada):
    rows = c_all.shape[0]
    n = w_ada.shape[1]
    bn = D_MODEL
    return pl.pallas_call(
        _ada_kernel,
        grid=(n // bn,),
        in_specs=[pl.BlockSpec((rows, D_MODEL), lambda j: (0, 0)),
                  pl.BlockSpec((D_MODEL, bn), lambda j: (0, j)),
                  pl.BlockSpec((1, bn), lambda j: (0, j))],
        out_specs=pl.BlockSpec((rows, bn), lambda j: (0, j)),
        out_shape=jax.ShapeDtypeStruct((rows, n), F32),
        compiler_params=_params(),
        name="ada",
    )(c_all, w_ada, b_ada.reshape(1, n))


def _mod_spec(per_row, tm, tpb, k):
    if per_row:
        return pl.BlockSpec((tm, D_MODEL), lambda i: (i, k))
    return pl.BlockSpec((None, 1, D_MODEL), lambda i: (i // tpb, 0, k))


def _ffn_kernel(*refs, premix, final):
    it = iter(refs)
    x_ref = next(it)
    if premix:
        ys_ref, of_ref, wo_ref, g2_ref = next(it), next(it), next(it), next(it)
    sh_ref, sc_ref, gt_ref, gn_ref, wup_ref, wdn_ref = (next(it) for _ in range(6))
    if final:
        gf_ref = next(it)
    o_ref = next(it)

    x = x_ref[...]
    if premix:
        mixed = _dot(ys_ref[...], wo_ref[:D_S5, :]) + _dot(of_ref[...], wo_ref[D_S5:, :])
        x = x + (1.0 + g2_ref[...]) * mixed
    h = _rms(x, gn_ref[...]) * (1.0 + sc_ref[...]) + sh_ref[...]
    gu = _dot(h.astype(BF16), wup_ref[...])
    g = gu[:, :D_FF]
    u = gu[:, D_FF:]
    a = (g * _sigmoid(g) * u).astype(BF16)
    y = _dot(a, wdn_ref[...])
    xn = x + (0.5 * (1.0 + gt_ref[...])) * y
    o_ref[...] = _rms(xn, gf_ref[...]) if final else xn


def _ffn_call(x, mod, k0, per_row, tm, tpb, gn, wup, wdn, *, premix=None, final_g=None, name):
    n = x.shape[0]
    tok = pl.BlockSpec((tm, D_MODEL), lambda i: (i, 0))
    args, specs = [x], [tok]
    if premix is not None:
        ys, ys_spec, of, wo, k2 = premix
        args += [ys, of, wo, mod]
        specs += [ys_spec, pl.BlockSpec((tm, D_FOX), lambda i: (i, 0)), _resident(wo.shape),
                  _mod_spec(per_row, tm, tpb, k2)]
    args += [mod, mod, mod, gn.reshape(1, D_MODEL), wup, wdn]
    specs += [_mod_spec(per_row, tm, tpb, k0), _mod_spec(per_row, tm, tpb, k0 + 1),
              _mod_spec(per_row, tm, tpb, k0 + 2), _resident((1, D_MODEL)),
              _resident(wup.shape), _resident(wdn.shape)]
    final = final_g is not None
    if final:
        args.append(final_g.reshape(1, D_MODEL))
        specs.append(_resident((1, D_MODEL)))
    return pl.pallas_call(
        functools.partial(_ffn_kernel, premix=premix is not None, final=final),
        grid=(n // tm,),
        in_specs=specs,
        out_specs=tok,
        out_shape=jax.ShapeDtypeStruct((n, D_MODEL), F32),
        compiler_params=_params(),
        name=name,
    )(*args)


def _inproj_kernel(x_ref, sh_ref, sc_ref, gn_ref, win_ref, wf_ref, bf_ref,
                   u_ref, qb_ref, kb_ref, vb_ref, k_ref, v_ref, lf_ref):
    h = (_rms(x_ref[...], gn_ref[...]) * (1.0 + sc_ref[...]) + sh_ref[...]).astype(BF16)
    proj = _dot(h, win_ref[...])
    u_ref[...] = proj[:, :D_S5]
    qb_ref[...] = (proj[:, D_S5:D_S5 + D_FOX] * ATTN_SCALE).astype(BF16)
    k = proj[:, D_S5 + D_FOX:D_S5 + 2 * D_FOX]
    v = proj[:, D_S5 + 2 * D_FOX:]
    k_ref[...] = k
    v_ref[...] = v
    kb_ref[...] = k.astype(BF16)
    vb_ref[...] = v.astype(BF16)
    z = _dot(h, wf_ref[...]) + bf_ref[...]
    lf = jnp.minimum(z, 0.0) - jnp.log1p(jnp.exp(-jnp.abs(z)))
    lf_ref[...] = lf[:, :FOX_HEADS]


def _inproj_call(x, mod, per_row, tm, tpb, gn, w_qkvu, w_f, b_f, u_shape, u_map, name):
    n = x.shape[0]
    tok = lambda w: pl.BlockSpec((tm, w), lambda i: (i, 0))
    sd = jax.ShapeDtypeStruct
    return pl.pallas_call(
        _inproj_kernel,
        grid=(n // tm,),
        in_specs=[tok(D_MODEL), _mod_spec(per_row, tm, tpb, 3), _mod_spec(per_row, tm, tpb, 4),
                  _resident((1, D_MODEL)), _resident(w_qkvu.shape), _resident(w_f.shape),
                  _resident((1, LANES))],
        out_specs=[pl.BlockSpec((tm, D_S5), u_map), tok(D_FOX), tok(D_FOX), tok(D_FOX),
                   tok(D_FOX), tok(D_FOX), tok(FOX_HEADS)],
        out_shape=[sd(u_shape, F32), sd((n, D_FOX), BF16), sd((n, D_FOX), BF16), sd((n, D_FOX), BF16),
                   sd((n, D_FOX), F32), sd((n, D_FOX), F32), sd((n, FOX_HEADS), F32)],
        compiler_params=_params(),
        name=name,
    )(x, mod, mod, gn.reshape(1, D_MODEL), w_qkvu, w_f, b_f)


CUM_CHUNK = 512


def _cumsum_kernel(x_ref, o_ref):
    t = x_ref.shape[-1]
    row = lax.broadcasted_iota(jnp.int32, (CUM_CHUNK, CUM_CHUNK), 0)
    col = lax.broadcasted_iota(jnp.int32, (CUM_CHUNK, CUM_CHUNK), 1)
    tri = jnp.where(row <= col, 1.0, 0.0).astype(BF16)
    carry = jnp.zeros((x_ref.shape[0], 1), F32)
    for c in range(t // CUM_CHUNK):
        sl = slice(c * CUM_CHUNK, (c + 1) * CUM_CHUNK)
        hi, mid, lo = _split3(x_ref[:, sl])
        cum = (_dot(hi, tri) + _dot(mid, tri)) + _dot(lo, tri) + carry
        o_ref[:, sl] = cum
        carry = cum[:, CUM_CHUNK - 1:CUM_CHUNK]


def _cumsum_call(x_t, name):
    b, h, t = x_t.shape
    spec = pl.BlockSpec((None, h, t), lambda i: (i, 0, 0))
    return pl.pallas_call(
        _cumsum_kernel, grid=(b,), in_specs=[spec], out_specs=spec,
        out_shape=jax.ShapeDtypeStruct((b, h, t), F32), compiler_params=_params(), name=name,
    )(x_t)


ATTN_TQ = 512


def _attn_kernel(q_ref, k_ref, v_ref, fq_ref, fk_ref, g_ref, o_ref, m_ref, l_ref, acc_ref, o_scr, *, tq):
    qi = pl.program_id(1)
    lane = lax.broadcasted_iota(jnp.int32, (tq, LANES), 1)
    lo_half = lane < FOX_HEAD_DIM
    row = lax.broadcasted_iota(jnp.int32, (tq, tq), 0)
    col = lax.broadcasted_iota(jnp.int32, (tq, tq), 1)
    causal = col <= row

    for hp in range(FOX_HEADS // 2):
        cs = slice(hp * LANES, (hp + 1) * LANES)
        q2 = q_ref[:, cs]
        zero = jnp.zeros_like(q2)
        qs = (jnp.where(lo_half, q2, zero), jnp.where(lo_half, zero, q2))
        fqs = tuple(fq_ref[:, 2 * hp + e:2 * hp + e + 1] for e in range(2))
        m_ref[...] = jnp.full(m_ref.shape, NEG_INF, F32)
        l_ref[...] = jnp.zeros(l_ref.shape, F32)
        acc_ref[...] = jnp.zeros(acc_ref.shape, F32)

        def block(j, masked):
            off = pl.multiple_of(j * tq, tq)
            k2 = k_ref[pl.ds(off, tq), cs]
            v2 = v_ref[pl.ds(off, tq), cs]
            for e in range(2):
                fk = fk_ref[2 * hp + e:2 * hp + e + 1, pl.ds(off, tq)]
                s = _dot_nt(qs[e], k2) + fqs[e] - fk
                if masked:
                    s = jnp.where(causal, s, NEG_INF)
                m_prev = m_ref[e]
                m_new = jnp.maximum(m_prev, jnp.max(s, axis=1, keepdims=True))
                alpha = jnp.exp(m_prev - m_new)
                p = jnp.exp(s - m_new)
                l_ref[e] = alpha * l_ref[e] + jnp.sum(p, axis=1, keepdims=True)
                acc_ref[e] = alpha * acc_ref[e] + _dot(p.astype(BF16), v2)
                m_ref[e] = m_new

        def body(j, c):
            block(j, False)
            return c

        lax.fori_loop(0, qi, body, 0)
        block(qi, True)
        o_scr[:, cs] = jnp.where(lo_half, acc_ref[0] / l_ref[0], acc_ref[1] / l_ref[1])

    o_ref[...] = _rms(o_scr[...], g_ref[...]).astype(BF16)


def _attn_call(qb, kb, vb, f_col, f_row, g_fox, batch, seq):
    tq = ATTN_TQ
    nq = seq // tq
    return pl.pallas_call(
        functools.partial(_attn_kernel, tq=tq),
        grid=(batch, nq),
        in_specs=[pl.BlockSpec((None, tq, D_FOX), lambda b, i: (b, i, 0)),
                  pl.BlockSpec((None, seq, D_FOX), lambda b, i: (b, 0, 0)),
                  pl.BlockSpec((None, seq, D_FOX), lambda b, i: (b, 0, 0)),
                  pl.BlockSpec((None, tq, FOX_HEADS), lambda b, i: (b, i, 0)),
                  pl.BlockSpec((None, FOX_HEADS, seq), lambda b, i: (b, 0, 0)),
                  pl.BlockSpec((1, D_FOX), lambda b, i: (0, 0))],
        out_specs=pl.BlockSpec((None, tq, D_FOX), lambda b, i: (b, i, 0)),
        out_shape=jax.ShapeDtypeStruct((batch, seq, D_FOX), BF16),
        scratch_shapes=[pltpu.VMEM((2, tq, 1), F32), pltpu.VMEM((2, tq, 1), F32),
                        pltpu.VMEM((2, tq, LANES), F32), pltpu.VMEM((tq, D_FOX), F32)],
        compiler_params=_params(2),
        name="fox_attn_prompt",
    )(qb.reshape(batch, seq, D_FOX), kb.reshape(batch, seq, D_FOX), vb.reshape(batch, seq, D_FOX),
      f_col, f_row, g_fox.reshape(1, D_FOX))


SAMPLE_KC = 1024


def _attn_sample_kernel(q_ref, kn_ref, vn_ref, ck_ref, cv_ref, fq_ref, fk_ref, g_ref, o_ref,
                        qbd_ref, m_ref, l_ref, acc_ref, *, nq, past):
    c = pl.program_id(1)
    nc = pl.num_programs(1)
    rows = FOX_HEADS * nq
    rblk = lax.broadcasted_iota(jnp.int32, (rows, D_FOX), 0) // nq
    cblk = lax.broadcasted_iota(jnp.int32, (rows, D_FOX), 1) // FOX_HEAD_DIM
    own_head = rblk == cblk

    @pl.when(c == 0)
    def _():
        qt = jnp.concatenate([q_ref[...]] * FOX_HEADS, axis=0)
        qbd_ref[...] = jnp.where(own_head, qt, jnp.zeros_like(qt))
        m_ref[...] = jnp.full(m_ref.shape, NEG_INF, F32)
        l_ref[...] = jnp.zeros(l_ref.shape, F32)
        acc_ref[...] = jnp.zeros(acc_ref.shape, F32)

    def expand_heads(f):
        n = f.shape[1]
        return jnp.concatenate([jnp.broadcast_to(f[h:h + 1, :], (nq, n)) for h in range(FOX_HEADS)], axis=0)

    def update(kb, vb, fk, mask):
        s = _dot_nt(qbd_ref[...], kb) + fq_ref[...] - expand_heads(fk)
        if mask is not None:
            s = jnp.where(mask, s, NEG_INF)
        m_prev = m_ref[...]
        m_new = jnp.maximum(m_prev, jnp.max(s, axis=1, keepdims=True))
        alpha = jnp.exp(m_prev - m_new)
        p = jnp.exp(s - m_new)
        l_ref[...] = alpha * l_ref[...] + jnp.sum(p, axis=1, keepdims=True)
        acc_ref[...] = alpha * acc_ref[...] + _dot(p.astype(BF16), vb)
        m_ref[...] = m_new

    off = pl.multiple_of(c * SAMPLE_KC, SAMPLE_KC)
    update(ck_ref[...].astype(BF16), cv_ref[...].astype(BF16), fk_ref[:, pl.ds(off, SAMPLE_KC)], None)

    @pl.when(c == nc - 1)
    def _():
        qpos = lax.broadcasted_iota(jnp.int32, (rows, nq), 0) % nq
        kpos = lax.broadcasted_iota(jnp.int32, (rows, nq), 1)
        update(kn_ref[...], vn_ref[...], fk_ref[:, past:past + nq], kpos <= qpos)
        o_full = jnp.where(own_head, acc_ref[...] / l_ref[...], 0.0)
        o = o_full[0:nq, :]
        for h in range(1, FOX_HEADS):
            o = o + o_full[h * nq:(h + 1) * nq, :]
        o_ref[...] = _rms(o, g_ref[...]).astype(BF16)


def _attn_sample_call(qb, kb, vb, cache_k, cache_v, fq_col, f_row, g_fox, batch, nq, past):
    rows = FOX_HEADS * nq
    nc = past // SAMPLE_KC
    per_b = lambda w: pl.BlockSpec((None, nq, w), lambda b, c: (b, 0, 0))
    cache = pl.BlockSpec((None, SAMPLE_KC, D_FOX), lambda b, c: (b, c, 0))
    return pl.pallas_call(
        functools.partial(_attn_sample_kernel, nq=nq, past=past),
        grid=(batch, nc),
        in_specs=[per_b(D_FOX), per_b(D_FOX), per_b(D_FOX), cache, cache,
                  pl.BlockSpec((None, rows, 1), lambda b, c: (b, 0, 0)),
                  pl.BlockSpec((None, FOX_HEADS, f_row.shape[-1]), lambda b, c: (b, 0, 0)),
                  pl.BlockSpec((1, D_FOX), lambda b, c: (0, 0))],
        out_specs=per_b(D_FOX),
        out_shape=jax.ShapeDtypeStruct((batch, nq, D_FOX), BF16),
        scratch_shapes=[pltpu.VMEM((rows, D_FOX), BF16), pltpu.VMEM((rows, 1), F32),
                        pltpu.VMEM((rows, 1), F32), pltpu.VMEM((rows, D_FOX), F32)],
        compiler_params=_params(2),
        name="fox_attn_sample",
    )(qb.reshape(batch, nq, D_FOX), kb.reshape(batch, nq, D_FOX), vb.reshape(batch, nq, D_FOX),
      cache_k, cache_v, fq_col, f_row, g_fox.reshape(1, D_FOX))


def _s5_prep_kernel(lre_ref, lim_ref, ldt_ref, bre_ref, bim_ref, cre_ref, cim_ref,
                    lre_row_ref, lim_row_ref, ldt_row_ref, bbd_ref, cbd_ref, acon_ref):
    def a_bar(lre, lim, ldt):
        dt = jnp.exp(ldt)
        mag = jnp.exp(lre * dt)
        return mag * jnp.cos(lim * dt), mag * jnp.sin(lim * dt)

    lre, lim = lre_ref[...], lim_ref[...]
    ar, ai = a_bar(lre, lim, ldt_ref[...])
    den = lre * lre + lim * lim
    cr = ((ar - 1.0) * lre + ai * lim) / den
    ci = (ai * lre - (ar - 1.0) * lim) / den
    bre, bim = bre_ref[...], bim_ref[...]
    b_r = (cr * bre - ci * bim).astype(BF16)
    b_i = (cr * bim + ci * bre).astype(BF16)
    srow = lax.broadcasted_iota(jnp.int32, (S5_STATE, N_STATE), 0)
    scol = lax.broadcasted_iota(jnp.int32, (S5_STATE, N_STATE), 1)
    rep_state = jnp.where(scol % S5_STATE == srow, 1.0, 0.0).astype(BF16)
    grow = lax.broadcasted_iota(jnp.int32, (D_S5, N_STATE), 0) // S5_GROUP
    gcol = lax.broadcasted_iota(jnp.int32, (D_S5, N_STATE), 1) // S5_STATE
    own = grow == gcol
    bbd_ref[:, :N_STATE] = jnp.where(own, _dot(b_r, rep_state), 0.0).astype(BF16)
    bbd_ref[:, N_STATE:] = jnp.where(own, _dot(b_i, rep_state), 0.0).astype(BF16)

    crow = lax.broadcasted_iota(jnp.int32, (S5_GROUP, D_S5), 0)
    ccol = lax.broadcasted_iota(jnp.int32, (S5_GROUP, D_S5), 1)
    rep_chan = jnp.where(ccol % S5_GROUP == crow, 1.0, 0.0).astype(BF16)
    grow2 = lax.broadcasted_iota(jnp.int32, (N_STATE, D_S5), 0) // S5_STATE
    gcol2 = lax.broadcasted_iota(jnp.int32, (N_STATE, D_S5), 1) // S5_GROUP
    own2 = grow2 == gcol2
    cbd_ref[:N_STATE, :] = jnp.where(own2, _dot(cre_ref[...].astype(BF16), rep_chan), 0.0).astype(BF16)
    cbd_ref[N_STATE:, :] = jnp.where(own2, -_dot(cim_ref[...].astype(BF16), rep_chan), 0.0).astype(BF16)

    ar1, ai1 = a_bar(lre_row_ref[...], lim_row_ref[...], ldt_row_ref[...])
    ar2 = ar1 * ar1 - ai1 * ai1
    ai2 = 2.0 * ar1 * ai1
    upper = lax.broadcasted_iota(jnp.int32, (SUBLANES, N_STATE), 0) < SUBLANES // 2
    bc = lambda v: jnp.broadcast_to(v, (SUBLANES, N_STATE))
    acon_ref[0] = bc(ar1)
    acon_ref[1] = bc(ai1)
    acon_ref[2] = jnp.where(upper, bc(ar1), bc(ar2))
    acon_ref[3] = jnp.where(upper, bc(ai1), bc(ai2))
    acon_ref[4] = jnp.where(upper, 0.0, bc(ar1))
    acon_ref[5] = jnp.where(upper, 0.0, bc(ai1))


def _s5_prep_call(lam_re, lam_im, log_dt, b_re, b_im, c_re, c_im):
    rep_rows = lambda a: jnp.repeat(a, S5_GROUP, axis=0)
    ldt_gp = jnp.broadcast_to(log_dt[:, None], (S5_GROUPS, S5_STATE))
    to_cp = lambda b: jnp.transpose(b, (0, 2, 1)).reshape(D_S5, S5_STATE)
    to_pc = lambda c: jnp.transpose(c, (0, 2, 1)).reshape(N_STATE, S5_GROUP)
    row = lambda a: a.reshape(1, N_STATE)
    args = (rep_rows(lam_re), rep_rows(lam_im), rep_rows(ldt_gp), to_cp(b_re), to_cp(b_im),
            to_pc(c_re), to_pc(c_im), row(lam_re), row(lam_im), row(ldt_gp))
    sd = jax.ShapeDtypeStruct
    return pl.pallas_call(
        _s5_prep_kernel,
        out_shape=[sd((D_S5, 2 * N_STATE), BF16), sd((2 * N_STATE, D_S5), BF16),
                   sd((6, SUBLANES, N_STATE), F32)],
        compiler_params=pltpu.CompilerParams(vmem_limit_bytes=VMEM_LIMIT),
        name="s5_prep",
    )(*args)


SCAN_COLS = 512


def _s5_kernel(u_ref, h0_ref, bbd_ref, cbd_ref, acon_ref, d_ref, wglu_ref, bglu_ref, gs_ref,
               y_ref, st_ref, s_ref, h_ref, *, nb):
    i = pl.program_id(0)
    rows = u_ref.shape[0]

    @pl.when(i == 0)
    def _():
        h_ref[...] = h0_ref[...]

    u = u_ref[...]
    s_ref[...] = _dot(u.astype(BF16), bbd_ref[...])

    upper = lax.broadcasted_iota(jnp.int32, (SUBLANES, SCAN_COLS), 0) < SUBLANES // 2
    for c in range(N_STATE // SCAN_COLS):
        re = slice(c * SCAN_COLS, (c + 1) * SCAN_COLS)
        im = slice(N_STATE + c * SCAN_COLS, N_STATE + (c + 1) * SCAN_COLS)
        if nb == SUBLANES:
            ar, ai = acon_ref[0, :, re], acon_ref[1, :, re]

            def step(k, carry):
                hr, hi = carry
                r0 = pl.multiple_of(k * SUBLANES, SUBLANES)
                nr = ar * hr - ai * hi + s_ref[pl.ds(r0, SUBLANES), re]
                ni = ar * hi + ai * hr + s_ref[pl.ds(r0, SUBLANES), im]
                s_ref[pl.ds(r0, SUBLANES), re] = nr
                s_ref[pl.ds(r0, SUBLANES), im] = ni
                return nr, ni
        else:
            a2r, a2i = acon_ref[2, :, re], acon_ref[3, :, re]
            alr, ali = acon_ref[4, :, re], acon_ref[5, :, re]

            def step(k, carry):
                hr, hi = carry
                r0 = pl.multiple_of(k * SUBLANES, SUBLANES)
                xr = s_ref[pl.ds(r0, SUBLANES), re]
                xi = s_ref[pl.ds(r0, SUBLANES), im]
                sr = pltpu.roll(xr, SUBLANES // 2, 0)
                si = pltpu.roll(xi, SUBLANES // 2, 0)
                nr = (a2r * hr - a2i * hi) + (alr * sr - ali * si) + xr
                ni = (a2r * hi + a2i * hr) + (alr * si + ali * sr) + xi
                s_ref[pl.ds(r0, SUBLANES), re] = nr
                s_ref[pl.ds(r0, SUBLANES), im] = ni
                return (jnp.where(upper, pltpu.roll(nr, SUBLANES // 2, 0), nr),
                        jnp.where(upper, pltpu.roll(ni, SUBLANES // 2, 0), ni))

        hr, hi = lax.fori_loop(0, rows // SUBLANES, step, (h_ref[0, :, re], h_ref[1, :, re]), unroll=2)
        h_ref[0, :, re] = hr
        h_ref[1, :, re] = hi

    @pl.when(i == pl.num_programs(0) - 1)
    def _():
        st_ref[...] = h_ref[...]

    y = _dot(s_ref[...].astype(BF16), cbd_ref[...]) + d_ref[...] * u
    y = 0.5 * y * (1.0 + jnp.tanh(math.sqrt(2.0 / math.pi) * (y + 0.044715 * (y * y * y))))
    y = y * _sigmoid(_dot(y.astype(BF16), wglu_ref[...]) + bglu_ref[...])
    y_ref[...] = _rms(y, gs_ref[...]).astype(BF16)


def _s5_call(u_tb, h0, bbd, cbd, acon, s5_d, wglu, b_glu, g_s5, nb, tt, name):
    n = u_tb.shape[0]
    rows = tt * nb
    blk = pl.BlockSpec((rows, D_S5), lambda i: (i, 0))
    vec = lambda a: a.reshape(1, D_S5)
    return pl.pallas_call(
        functools.partial(_s5_kernel, nb=nb),
        grid=(n // rows,),
        in_specs=[blk, _resident(h0.shape), _resident(bbd.shape), _resident(cbd.shape),
                  _resident(acon.shape), _resident((1, D_S5)), _resident(wglu.shape),
                  _resident((1, D_S5)), _resident((1, D_S5))],
        out_specs=[blk, pl.BlockSpec(h0.shape, lambda i: (0, 0, 0))],
        out_shape=[jax.ShapeDtypeStruct((n, D_S5), BF16), jax.ShapeDtypeStruct(h0.shape, F32)],
        scratch_shapes=[pltpu.VMEM((rows, 2 * N_STATE), F32), pltpu.VMEM(h0.shape, F32)],
        compiler_params=_params(),
        name=name,
    )(u_tb, h0, bbd, cbd, acon, vec(s5_d), wglu, vec(b_glu), vec(g_s5))


FFN_TM = 512
S5_TT = 128


def kernel(x_prompt, x_sample, c_prompt, c_sample, cache_fox_k, cache_fox_v, cache_fox_logf, state_s5_re, state_s5_im, w_ada, b_ada, g_ffn1, w_up1, w_down1, g_mix, w_in, b_fgate, lam_re, lam_im, log_dt, s5_b_re, s5_b_im, s5_c_re, s5_c_im, s5_d, w_glu, b_glu, g_s5_out, g_fox_out, w_out, g_ffn2, w_up2, w_down2, g_final):
    assert w_ada.shape[0] == 1, "single-layer trunk"
    bp, seq, _ = x_prompt.shape
    bs, nq, _ = x_sample.shape
    past = cache_fox_k.shape[2]
    n_p, n_s = bp * seq, bs * nq

    wup1, wdn1 = w_up1[0].astype(BF16), w_down1[0].astype(BF16)
    wup2, wdn2 = w_up2[0].astype(BF16), w_down2[0].astype(BF16)
    w_qkvu = w_in[0][:, :D_S5 + 3 * D_FOX].astype(BF16)
    w_f = jnp.pad(w_in[0][:, D_S5 + 3 * D_FOX:], ((0, 0), (0, LANES - FOX_HEADS))).astype(BF16)
    b_f = jnp.pad(b_fgate[0], (0, LANES - FOX_HEADS)).reshape(1, LANES)
    wo = w_out[0].astype(BF16)
    wglu = w_glu[0].astype(BF16)

    pad_rows = -(bp + bs) % SUBLANES
    c_all = jnp.concatenate([c_prompt, c_sample, jnp.zeros((pad_rows, D_MODEL), F32)], axis=0)
    mod = _ada_call(c_all, w_ada[0], b_ada[0])
    mod_p = mod[:bp].reshape(bp, 1, N_MOD * D_MODEL)
    mod_s = jnp.repeat(mod[bp:bp + bs], nq, axis=0)

    bbd, cbd, acon = _s5_prep_call(lam_re[0], lam_im[0], log_dt[0], s5_b_re[0], s5_b_im[0],
                                   s5_c_re[0], s5_c_im[0])

    tm = FFN_TM
    tpb = seq // tm
    xp = x_prompt.reshape(n_p, D_MODEL)
    x1 = _ffn_call(xp, mod_p, 0, False, tm, tpb, g_ffn1[0], wup1, wdn1, name="ffn1_prompt")
    u_tb, qb, kb, vb, k_p, v_p, lf_p = _inproj_call(
        x1, mod_p, False, tm, tpb, g_mix[0], w_qkvu, w_f, b_f,
        (seq, bp * D_S5), lambda i: (i % tpb, i // tpb), "inproj_prompt")
    lf_p = lf_p.reshape(bp, seq, FOX_HEADS)
    f_row = _cumsum_call(jnp.transpose(lf_p, (0, 2, 1)), "cumsum_prompt")
    f_col = jnp.transpose(f_row, (0, 2, 1))
    o_p = _attn_call(qb, kb, vb, f_col, f_row, g_fox_out[0], bp, seq)
    h0_p = jnp.zeros((2, SUBLANES, N_STATE), F32)
    ys_tb, st_p = _s5_call(u_tb.reshape(seq * bp, D_S5), h0_p, bbd, cbd, acon, s5_d[0], wglu, b_glu[0],
                           g_s5_out[0], bp, S5_TT, "s5_prompt")
    ys_spec = pl.BlockSpec((tm, D_S5), lambda i: (i % tpb, i // tpb))
    y_p = _ffn_call(x1, mod_p, 6, False, tm, tpb, g_ffn2[0], wup2, wdn2,
                       premix=(ys_tb.reshape(seq, bp * D_S5), ys_spec, o_p.reshape(n_p, D_FOX), wo, 5),
                       final_g=g_final, name="ffn2_prompt")

    xs = x_sample.reshape(n_s, D_MODEL)
    xs1 = _ffn_call(xs, mod_s, 0, True, n_s, 1, g_ffn1[0], wup1, wdn1, name="ffn1_sample")
    u_s, qb_s, kb_s, vb_s, k_s, v_s, lf_s = _inproj_call(
        xs1, mod_s, True, n_s, 1, g_mix[0], w_qkvu, w_f, b_f, (n_s, D_S5), lambda i: (i, 0), "inproj_sample")
    lf_s = lf_s.reshape(bs, nq, FOX_HEADS)
    f_pad = -(past + nq) % CUM_CHUNK
    lf_all = jnp.concatenate([cache_fox_logf[0], lf_s, jnp.zeros((bs, f_pad, FOX_HEADS), F32)], axis=1)
    f_row_s = _cumsum_call(jnp.transpose(lf_all, (0, 2, 1)), "cumsum_sample")
    fq_col = f_row_s[:, :, past:past + nq].reshape(bs, FOX_HEADS * nq, 1)
    o_s = _attn_sample_call(qb_s, kb_s, vb_s, cache_fox_k[0].reshape(bs, past, D_FOX),
                            cache_fox_v[0].reshape(bs, past, D_FOX), fq_col, f_row_s, g_fox_out[0],
                            bs, nq, past)
    u_s_tb = jnp.transpose(u_s.reshape(bs, nq, D_S5), (1, 0, 2)).reshape(n_s, D_S5)
    h0_s = jnp.stack([state_s5_re[0].reshape(bs, N_STATE), state_s5_im[0].reshape(bs, N_STATE)])
    ys_s_tb, st_s = _s5_call(u_s_tb, h0_s, bbd, cbd, acon, s5_d[0], wglu, b_glu[0], g_s5_out[0],
                             bs, nq, "s5_sample")
    ys_s = jnp.transpose(ys_s_tb.reshape(nq, bs, D_S5), (1, 0, 2)).reshape(n_s, D_S5)
    y_s = _ffn_call(xs1, mod_s, 6, True, n_s, 1, g_ffn2[0], wup2, wdn2,
                       premix=(ys_s, pl.BlockSpec((n_s, D_S5), lambda i: (i, 0)), o_s.reshape(n_s, D_FOX), wo, 5),
                       final_g=g_final, name="ffn2_sample")

    heads = lambda a, b, t: a.reshape(1, b, t, FOX_HEADS, FOX_HEAD_DIM)
    state = lambda a, b: a[:b].reshape(1, b, S5_GROUPS, S5_STATE)
    return (y_p.reshape(bp, seq, D_MODEL), y_s.reshape(bs, nq, D_MODEL),
            heads(k_p, bp, seq), heads(v_p, bp, seq), lf_p.reshape(1, bp, seq, FOX_HEADS),
            state(st_p[0], bp), state(st_p[1], bp),
            heads(k_s, bs, nq), heads(v_s, bs, nq), lf_s.reshape(1, bs, nq, FOX_HEADS),
            state(st_s[0], bs), state(st_s[1], bs))
```

```python
import functools
import math

import jax
import jax.numpy as jnp
from jax import lax
from jax.experimental import pallas as pl
from jax.experimental.pallas import tpu as pltpu

F32 = jnp.float32
BF16 = jnp.bfloat16

D_MODEL = 1024
D_FF = 2816
D_S5 = 512
D_FOX = 512
FOX_HEADS = 8
FOX_HEAD_DIM = 64
S5_GROUPS = 32
S5_GROUP = 16
S5_STATE = 64
N_STATE = S5_GROUPS * S5_STATE
N_MOD = 9
ATTN_SCALE = FOX_HEAD_DIM ** -0.5
NEG_INF = -1e30
EPS = 1e-6
LANES = 128
SUBLANES = 8
VMEM_LIMIT = 56 * 1024 * 1024


def _params(n_axes=1, vmem=VMEM_LIMIT):
    return pltpu.CompilerParams(dimension_semantics=("arbitrary",) * n_axes, vmem_limit_bytes=vmem)


def _resident(shape):
    nd = len(shape)
    return pl.BlockSpec(shape, lambda *_: (0,) * nd, pipeline_mode=pl.Buffered(1))


def _sigmoid(x):
    return 1.0 / (1.0 + jnp.exp(-x))


def _rms(x, g):
    y = x * lax.rsqrt(jnp.mean(x * x, axis=-1, keepdims=True) + EPS)
    return y * g


def _dot(a, b):
    return jnp.dot(a, b, preferred_element_type=F32)


def _dot_nt(a, b):
    return lax.dot_general(a, b, (((1,), (1,)), ((), ())), preferred_element_type=F32)


def _split3(x):
    hi = x.astype(BF16)
    r = x - hi.astype(F32)
    mid = r.astype(BF16)
    lo = (r - mid.astype(F32)).astype(BF16)
    return hi, mid, lo


def _ada_kernel(c_ref, w_ref, b_ref, o_ref):
    c = c_ref[...]
    s = (c * _sigmoid(c)).astype(BF16)
    o_ref[...] = _dot(s, w_ref[...].astype(BF16)) + b_ref[...]


def _ada_call(c_all, w_ada, b_ada):
    rows = c_all.shape[0]
    n = w_ada.shape[1]
    bn = D_MODEL
    return pl.pallas_call(
        _ada_kernel,
        grid=(n // bn,),
        in_specs=[pl.BlockSpec((rows, D_MODEL), lambda j: (0, 0)),
                  pl.BlockSpec((D_MODEL, bn), lambda j: (0, j)),
                  pl.BlockSpec((1, bn), lambda j: (0, j))],
        out_specs=pl.BlockSpec((rows, bn), lambda j: (0, j)),
        out_shape=jax.ShapeDtypeStruct((rows, n), F32),
        compiler_params=_params(),
        name="ada",
    )(c_all, w_ada, b_ada.reshape(1, n))


def _mod_spec(per_row, tm, tpb, k):
    if per_row:
        return pl.BlockSpec((tm, D_MODEL), lambda i: (i, k))
    return pl.BlockSpec((None, 1, D_MODEL), lambda i: (i // tpb, 0, k))


def _ffn_kernel(*refs, premix, final):
    it = iter(refs)
    x_ref = next(it)
    if premix:
        ys_ref, of_ref, wo_ref, g2_ref = next(it), next(it), next(it), next(it)
    sh_ref, sc_ref, gt_ref, gn_ref, wup_ref, wdn_ref = (next(it) for _ in range(6))
    if final:
        gf_ref = next(it)
    o_ref = next(it)

    x = x_ref[...]
    if premix:
        mixed = _dot(ys_ref[...], wo_ref[:D_S5, :]) + _dot(of_ref[...], wo_ref[D_S5:, :])
        x = x + (1.0 + g2_ref[...]) * mixed
    h = _rms(x, gn_ref[...]) * (1.0 + sc_ref[...]) + sh_ref[...]
    gu = _dot(h.astype(BF16), wup_ref[...])
    g = gu[:, :D_FF]
    u = gu[:, D_FF:]
    a = (g * _sigmoid(g) * u).astype(BF16)
    y = _dot(a, wdn_ref[...])
    xn = x + (0.5 * (1.0 + gt_ref[...])) * y
    o_ref[...] = _rms(xn, gf_ref[...]) if final else xn


def _ffn_call(x, mod, k0, per_row, tm, tpb, gn, wup, wdn, *, premix=None, final_g=None, name):
    n = x.shape[0]
    tok = pl.BlockSpec((tm, D_MODEL), lambda i: (i, 0))
    args, specs = [x], [tok]
    if premix is not None:
        ys, ys_spec, of, wo, k2 = premix
        args += [ys, of, wo, mod]
        specs += [ys_spec, pl.BlockSpec((tm, D_FOX), lambda i: (i, 0)), _resident(wo.shape),
                  _mod_spec(per_row, tm, tpb, k2)]
    args += [mod, mod, mod, gn.reshape(1, D_MODEL), wup, wdn]
    specs += [_mod_spec(per_row, tm, tpb, k0), _mod_spec(per_row, tm, tpb, k0 + 1),
              _mod_spec(per_row, tm, tpb, k0 + 2), _resident((1, D_MODEL)),
              _resident(wup.shape), _resident(wdn.shape)]
    final = final_g is not None
    if final:
        args.append(final_g.reshape(1, D_MODEL))
        specs.append(_resident((1, D_MODEL)))
    return pl.pallas_call(
        functools.partial(_ffn_kernel, premix=premix is not None, final=final),
        grid=(n // tm,),
        in_specs=specs,
        out_specs=tok,
        out_shape=jax.ShapeDtypeStruct((n, D_MODEL), F32),
        compiler_params=_params(),
        name=name,
    )(*args)


F_ROWS = 16


def _inproj_kernel(x_ref, sh_ref, sc_ref, gn_ref, wn_ref, wt_ref, bf_ref, *out_refs, prompt):
    h = (_rms(x_ref[...], gn_ref[...]) * (1.0 + sc_ref[...]) + sh_ref[...]).astype(BF16)
    nat = _dot(h, wn_ref[...])
    pt = _dot_nt(wt_ref[...], h)
    q_t, k_t, v_t = pt[:D_FOX], pt[D_FOX:2 * D_FOX], pt[2 * D_FOX:3 * D_FOX]
    z = pt[3 * D_FOX:] + bf_ref[...]
    lf = (jnp.minimum(z, 0.0) - jnp.log1p(jnp.exp(-jnp.abs(z))))[:FOX_HEADS]
    if prompt:
        u_ref, qtb_ref, kb_ref, vtb_ref, kt_ref, vt_ref, lf_ref = out_refs
        u_ref[...] = nat[:, :D_S5]
        qtb_ref[...] = (q_t * ATTN_SCALE).astype(BF16)
        kb_ref[...] = nat[:, D_S5:].astype(BF16)
        vtb_ref[...] = v_t.astype(BF16)
        kt_ref[...] = k_t
        vt_ref[...] = v_t
    else:
        u_ref, qb_ref, ktb_ref, vtb_ref, k_ref, v_ref, lf_ref = out_refs
        u_ref[...] = nat[:, :D_S5]
        qb_ref[...] = (nat[:, D_S5:D_S5 + D_FOX] * ATTN_SCALE).astype(BF16)
        ktb_ref[...] = k_t.astype(BF16)
        vtb_ref[...] = v_t.astype(BF16)
        k_ref[...] = nat[:, D_S5 + D_FOX:D_S5 + 2 * D_FOX]
        v_ref[...] = nat[:, D_S5 + 2 * D_FOX:]
    lf_ref[...] = lf


def _inproj_call(x, mod, per_row, tm, tpb, gn, w_nat, w_t, b_f, prompt, name):
    n = x.shape[0]
    nb = n // (tm * tpb)
    seq = tm * tpb
    sd = jax.ShapeDtypeStruct
    tok = lambda w: pl.BlockSpec((tm, w), lambda i: (i, 0))
    if prompt:
        feat = lambda r: pl.BlockSpec((None, r, tm), lambda i: (i // tpb, 0, i % tpb))
        fshape = lambda r, dt: sd((nb, r, seq), dt)
        out_specs = [pl.BlockSpec((tm, D_S5), lambda i: (i % tpb, i // tpb)), feat(D_FOX), tok(D_FOX),
                     feat(D_FOX), feat(D_FOX), feat(D_FOX), feat(FOX_HEADS)]
        out_shape = [sd((seq, nb * D_S5), F32), fshape(D_FOX, BF16), sd((n, D_FOX), BF16),
                     fshape(D_FOX, BF16), fshape(D_FOX, F32), fshape(D_FOX, F32), fshape(FOX_HEADS, F32)]
    else:
        feat = lambda r: pl.BlockSpec((r, tm), lambda i: (0, i))
        out_specs = [tok(D_S5), tok(D_FOX), feat(D_FOX), feat(D_FOX), tok(D_FOX), tok(D_FOX), feat(FOX_HEADS)]
        out_shape = [sd((n, D_S5), F32), sd((n, D_FOX), BF16), sd((D_FOX, n), BF16), sd((D_FOX, n), BF16),
                     sd((n, D_FOX), F32), sd((n, D_FOX), F32), sd((FOX_HEADS, n), F32)]
    return pl.pallas_call(
        functools.partial(_inproj_kernel, prompt=prompt),
        grid=(n // tm,),
        in_specs=[tok(D_MODEL), _mod_spec(per_row, tm, tpb, 3), _mod_spec(per_row, tm, tpb, 4),
                  _resident((1, D_MODEL)), _resident(w_nat.shape), _resident(w_t.shape),
                  _resident((F_ROWS, 1))],
        out_specs=out_specs,
        out_shape=out_shape,
        compiler_params=_params(),
        name=name,
    )(x, mod, mod, gn.reshape(1, D_MODEL), w_nat, w_t, b_f)


CUM_CHUNK = 512


def _cumsum_kernel(x_ref, o_ref):
    t = x_ref.shape[-1]
    row = lax.broadcasted_iota(jnp.int32, (CUM_CHUNK, CUM_CHUNK), 0)
    col = lax.broadcasted_iota(jnp.int32, (CUM_CHUNK, CUM_CHUNK), 1)
    tri = jnp.where(row <= col, 1.0, 0.0).astype(BF16)
    carry = jnp.zeros((x_ref.shape[0], 1), F32)
    for c in range(t // CUM_CHUNK):
        sl = slice(c * CUM_CHUNK, (c + 1) * CUM_CHUNK)
        hi, mid, lo = _split3(x_ref[:, sl])
        cum = (_dot(hi, tri) + _dot(mid, tri)) + _dot(lo, tri) + carry
        o_ref[:, sl] = cum
        carry = cum[:, CUM_CHUNK - 1:CUM_CHUNK]


def _cumsum_call(x_t, name):
    b, h, t = x_t.shape
    spec = pl.BlockSpec((None, h, t), lambda i: (i, 0, 0))
    return pl.pallas_call(
        _cumsum_kernel, grid=(b,), in_specs=[spec], out_specs=spec,
        out_shape=jax.ShapeDtypeStruct((b, h, t), F32), compiler_params=_params(), name=name,
    )(x_t)


ATTN_TQ = 512


def _attn_kernel(qt_ref, k_ref, vt_ref, fq_ref, fk_ref, g_ref, o_ref, m_ref, l_ref, acc_ref, ot_scr, *, tq):
    qi = pl.program_id(1)
    pair = 2 * FOX_HEAD_DIM
    head0 = lax.broadcasted_iota(jnp.int32, (pair, tq), 0) < FOX_HEAD_DIM
    key = lax.broadcasted_iota(jnp.int32, (tq, tq), 0)
    qry = lax.broadcasted_iota(jnp.int32, (tq, tq), 1)
    causal = key <= qry

    for hp in range(FOX_HEADS // 2):
        fs = slice(hp * pair, (hp + 1) * pair)
        q2 = qt_ref[fs, :]
        zero = jnp.zeros_like(q2)
        qts = (jnp.where(head0, q2, zero), jnp.where(head0, zero, q2))
        fqs = tuple(fq_ref[2 * hp + e:2 * hp + e + 1, :] for e in range(2))
        m_ref[...] = jnp.full(m_ref.shape, NEG_INF, F32)
        l_ref[...] = jnp.zeros(l_ref.shape, F32)
        acc_ref[...] = jnp.zeros(acc_ref.shape, F32)

        def block(j, masked):
            off = pl.multiple_of(j * tq, tq)
            k2 = k_ref[pl.ds(off, tq), fs]
            for e in range(2):
                hs = slice(e * FOX_HEAD_DIM, (e + 1) * FOX_HEAD_DIM)
                vt = vt_ref[hp * pair + e * FOX_HEAD_DIM:hp * pair + (e + 1) * FOX_HEAD_DIM, pl.ds(off, tq)]
                fk = fk_ref[pl.ds(off, tq), 2 * hp + e:2 * hp + e + 1]
                r = _dot(k2, qts[e]) - fk
                if masked:
                    r = jnp.where(causal, r, NEG_INF)
                m_prev = m_ref[e]
                m_new = jnp.maximum(m_prev, jnp.max(r, axis=0, keepdims=True) + fqs[e])
                alpha = jnp.exp(m_prev - m_new)
                p = jnp.exp(r + (fqs[e] - m_new))
                l_ref[e] = alpha * l_ref[e] + jnp.sum(p, axis=0, keepdims=True)
                acc_ref[hs, :] = alpha * acc_ref[hs, :] + _dot(vt, p.astype(BF16))
                m_ref[e] = m_new

        def body(j, c):
            block(j, False)
            return c

        lax.fori_loop(0, qi, body, 0)
        block(qi, True)
        for e in range(2):
            hs = slice(e * FOX_HEAD_DIM, (e + 1) * FOX_HEAD_DIM)
            ot_scr[hp * pair + e * FOX_HEAD_DIM:hp * pair + (e + 1) * FOX_HEAD_DIM, :] = acc_ref[hs, :] / l_ref[e]

    ot = ot_scr[...]
    y = ot * lax.rsqrt(jnp.mean(ot * ot, axis=0, keepdims=True) + EPS) * g_ref[...]
    o_ref[...] = y.T.astype(BF16)


def _attn_call(qtb, kb, vtb, f_col, f_row, g_fox, batch, seq):
    tq = ATTN_TQ
    nq = seq // tq
    whole = lambda r, c: pl.BlockSpec((None, r, c), lambda b, i: (b, 0, 0), pipeline_mode=pl.Buffered(1))
    return pl.pallas_call(
        functools.partial(_attn_kernel, tq=tq),
        grid=(batch, nq),
        in_specs=[pl.BlockSpec((None, D_FOX, tq), lambda b, i: (b, 0, i)),
                  whole(seq, D_FOX), whole(D_FOX, seq),
                  pl.BlockSpec((None, FOX_HEADS, tq), lambda b, i: (b, 0, i)),
                  whole(seq, FOX_HEADS),
                  pl.BlockSpec((D_FOX, 1), lambda b, i: (0, 0))],
        out_specs=pl.BlockSpec((None, tq, D_FOX), lambda b, i: (b, i, 0)),
        out_shape=jax.ShapeDtypeStruct((batch, seq, D_FOX), BF16),
        scratch_shapes=[pltpu.VMEM((2, 1, tq), F32), pltpu.VMEM((2, 1, tq), F32),
                        pltpu.VMEM((2 * FOX_HEAD_DIM, tq), F32), pltpu.VMEM((D_FOX, tq), F32)],
        compiler_params=_params(2),
        name="fox_attn_prompt",
    )(qtb, kb.reshape(batch, seq, D_FOX), vtb, f_row, f_col, g_fox.reshape(D_FOX, 1))


SAMPLE_KC = 1024


def _attn_sample_kernel(q_ref, kn_ref, vn_ref, ck_ref, cv_ref, fq_ref, fk_ref, g_ref, o_ref,
                        qbd_ref, m_ref, l_ref, acc_ref, *, nq, past):
    c = pl.program_id(1)
    nc = pl.num_programs(1)
    rows = FOX_HEADS * nq
    rblk = lax.broadcasted_iota(jnp.int32, (rows, D_FOX), 0) // nq
    cblk = lax.broadcasted_iota(jnp.int32, (rows, D_FOX), 1) // FOX_HEAD_DIM
    own_head = rblk == cblk

    @pl.when(c == 0)
    def _():
        qt = jnp.concatenate([q_ref[...]] * FOX_HEADS, axis=0)
        qbd_ref[...] = jnp.where(own_head, qt, jnp.zeros_like(qt))
        m_ref[...] = jnp.full(m_ref.shape, NEG_INF, F32)
        l_ref[...] = jnp.zeros(l_ref.shape, F32)
        acc_ref[...] = jnp.zeros(acc_ref.shape, F32)

    def expand_heads(f):
        n = f.shape[1]
        return jnp.concatenate([jnp.broadcast_to(f[h:h + 1, :], (nq, n)) for h in range(FOX_HEADS)], axis=0)

    def update(kt, vt, fk, mask):
        s = _dot(qbd_ref[...], kt) + fq_ref[...] - expand_heads(fk)
        if mask is not None:
            s = jnp.where(mask, s, NEG_INF)
        m_prev = m_ref[...]
        m_new = jnp.maximum(m_prev, jnp.max(s, axis=1, keepdims=True))
        alpha = jnp.exp(m_prev - m_new)
        p = jnp.exp(s - m_new)
        l_ref[...] = alpha * l_ref[...] + jnp.sum(p, axis=1, keepdims=True)
        acc_ref[...] = alpha * acc_ref[...] + _dot_nt(p.astype(BF16), vt)
        m_ref[...] = m_new

    off = pl.multiple_of(c * SAMPLE_KC, SAMPLE_KC)
    update(ck_ref[...].astype(BF16), cv_ref[...].astype(BF16), fk_ref[:, pl.ds(off, SAMPLE_KC)], None)

    @pl.when(c == nc - 1)
    def _():
        qpos = lax.broadcasted_iota(jnp.int32, (rows, nq), 0) % nq
        kpos = lax.broadcasted_iota(jnp.int32, (rows, nq), 1)
        update(kn_ref[...], vn_ref[...], fk_ref[:, past:past + nq], kpos <= qpos)
        o_full = jnp.where(own_head, acc_ref[...] / l_ref[...], 0.0)
        o = o_full[0:nq, :]
        for h in range(1, FOX_HEADS):
            o = o + o_full[h * nq:(h + 1) * nq, :]
        o_ref[...] = _rms(o, g_ref[...]).astype(BF16)


def _attn_sample_call(qb, ktb, vtb, cache_kt, cache_vt, fq_col, f_row, g_fox, batch, nq, past):
    rows = FOX_HEADS * nq
    nc = past // SAMPLE_KC
    per_b = lambda w: pl.BlockSpec((None, nq, w), lambda b, c: (b, 0, 0))
    new_t = pl.BlockSpec((None, D_FOX, nq), lambda b, c: (b, 0, 0))
    cache = pl.BlockSpec((None, D_FOX, SAMPLE_KC), lambda b, c: (b, 0, c))
    return pl.pallas_call(
        functools.partial(_attn_sample_kernel, nq=nq, past=past),
        grid=(batch, nc),
        in_specs=[per_b(D_FOX), new_t, new_t, cache, cache,
                  pl.BlockSpec((None, rows, 1), lambda b, c: (b, 0, 0)),
                  pl.BlockSpec((None, FOX_HEADS, f_row.shape[-1]), lambda b, c: (b, 0, 0)),
                  pl.BlockSpec((1, D_FOX), lambda b, c: (0, 0))],
        out_specs=per_b(D_FOX),
        out_shape=jax.ShapeDtypeStruct((batch, nq, D_FOX), BF16),
        scratch_shapes=[pltpu.VMEM((rows, D_FOX), BF16), pltpu.VMEM((rows, 1), F32),
                        pltpu.VMEM((rows, 1), F32), pltpu.VMEM((rows, D_FOX), F32)],
        compiler_params=_params(2),
        name="fox_attn_sample",
    )(qb.reshape(batch, nq, D_FOX), ktb, vtb, cache_kt, cache_vt, fq_col, f_row, g_fox.reshape(1, D_FOX))


def _s5_prep_kernel(lre_ref, lim_ref, ldt_ref, bre_ref, bim_ref, cre_ref, cim_ref,
                    lre_row_ref, lim_row_ref, ldt_row_ref, bbd_ref, cbd_ref, acon_ref):
    def a_bar(lre, lim, ldt):
        dt = jnp.exp(ldt)
        mag = jnp.exp(lre * dt)
        return mag * jnp.cos(lim * dt), mag * jnp.sin(lim * dt)

    lre, lim = lre_ref[...], lim_ref[...]
    ar, ai = a_bar(lre, lim, ldt_ref[...])
    den = lre * lre + lim * lim
    cr = ((ar - 1.0) * lre + ai * lim) / den
    ci = (ai * lre - (ar - 1.0) * lim) / den
    bre, bim = bre_ref[...], bim_ref[...]
    b_r = (cr * bre - ci * bim).astype(BF16)
    b_i = (cr * bim + ci * bre).astype(BF16)
    srow = lax.broadcasted_iota(jnp.int32, (S5_STATE, N_STATE), 0)
    scol = lax.broadcasted_iota(jnp.int32, (S5_STATE, N_STATE), 1)
    rep_state = jnp.where(scol % S5_STATE == srow, 1.0, 0.0).astype(BF16)
    grow = lax.broadcasted_iota(jnp.int32, (D_S5, N_STATE), 0) // S5_GROUP
    gcol = lax.broadcasted_iota(jnp.int32, (D_S5, N_STATE), 1) // S5_STATE
    own = grow == gcol
    bbd_ref[:, :N_STATE] = jnp.where(own, _dot(b_r, rep_state), 0.0).astype(BF16)
    bbd_ref[:, N_STATE:] = jnp.where(own, _dot(b_i, rep_state), 0.0).astype(BF16)

    crow = lax.broadcasted_iota(jnp.int32, (S5_GROUP, D_S5), 0)
    ccol = lax.broadcasted_iota(jnp.int32, (S5_GROUP, D_S5), 1)
    rep_chan = jnp.where(ccol % S5_GROUP == crow, 1.0, 0.0).astype(BF16)
    grow2 = lax.broadcasted_iota(jnp.int32, (N_STATE, D_S5), 0) // S5_STATE
    gcol2 = lax.broadcasted_iota(jnp.int32, (N_STATE, D_S5), 1) // S5_GROUP
    own2 = grow2 == gcol2
    cbd_ref[:N_STATE, :] = jnp.where(own2, _dot(cre_ref[...].astype(BF16), rep_chan), 0.0).astype(BF16)
    cbd_ref[N_STATE:, :] = jnp.where(own2, -_dot(cim_ref[...].astype(BF16), rep_chan), 0.0).astype(BF16)

    ar1, ai1 = a_bar(lre_row_ref[...], lim_row_ref[...], ldt_row_ref[...])
    ar2 = ar1 * ar1 - ai1 * ai1
    ai2 = 2.0 * ar1 * ai1
    upper = lax.broadcasted_iota(jnp.int32, (SUBLANES, N_STATE), 0) < SUBLANES // 2
    bc = lambda v: jnp.broadcast_to(v, (SUBLANES, N_STATE))
    acon_ref[0] = bc(ar1)
    acon_ref[1] = bc(ai1)
    acon_ref[2] = jnp.where(upper, bc(ar1), bc(ar2))
    acon_ref[3] = jnp.where(upper, bc(ai1), bc(ai2))
    acon_ref[4] = jnp.where(upper, 0.0, bc(ar1))
    acon_ref[5] = jnp.where(upper, 0.0, bc(ai1))


def _s5_prep_call(lam_re, lam_im, log_dt, b_re, b_im, c_re, c_im):
    rep_rows = lambda a: jnp.repeat(a, S5_GROUP, axis=0)
    ldt_gp = jnp.broadcast_to(log_dt[:, None], (S5_GROUPS, S5_STATE))
    to_cp = lambda b: jnp.transpose(b, (0, 2, 1)).reshape(D_S5, S5_STATE)
    to_pc = lambda c: jnp.transpose(c, (0, 2, 1)).reshape(N_STATE, S5_GROUP)
    row = lambda a: a.reshape(1, N_STATE)
    args = (rep_rows(lam_re), rep_rows(lam_im), rep_rows(ldt_gp), to_cp(b_re), to_cp(b_im),
            to_pc(c_re), to_pc(c_im), row(lam_re), row(lam_im), row(ldt_gp))
    sd = jax.ShapeDtypeStruct
    return pl.pallas_call(
        _s5_prep_kernel,
        out_shape=[sd((D_S5, 2 * N_STATE), BF16), sd((2 * N_STATE, D_S5), BF16),
                   sd((6, SUBLANES, N_STATE), F32)],
        compiler_params=pltpu.CompilerParams(vmem_limit_bytes=VMEM_LIMIT),
        name="s5_prep",
    )(*args)


SCAN_COLS = 512


def _s5_kernel(u_ref, h0_ref, bbd_ref, cbd_ref, acon_ref, d_ref, wglu_ref, bglu_ref, gs_ref,
               y_ref, st_ref, s_ref, h_ref, *, nb):
    i = pl.program_id(0)
    rows = u_ref.shape[0]

    @pl.when(i == 0)
    def _():
        h_ref[...] = h0_ref[...]

    u = u_ref[...]
    s_ref[...] = _dot(u.astype(BF16), bbd_ref[...])

    upper = lax.broadcasted_iota(jnp.int32, (SUBLANES, SCAN_COLS), 0) < SUBLANES // 2
    for c in range(N_STATE // SCAN_COLS):
        re = slice(c * SCAN_COLS, (c + 1) * SCAN_COLS)
        im = slice(N_STATE + c * SCAN_COLS, N_STATE + (c + 1) * SCAN_COLS)
        if nb == SUBLANES:
            ar, ai = acon_ref[0, :, re], acon_ref[1, :, re]

            def step(k, carry):
                hr, hi = carry
                r0 = pl.multiple_of(k * SUBLANES, SUBLANES)
                nr = ar * hr - ai * hi + s_ref[pl.ds(r0, SUBLANES), re]
                ni = ar * hi + ai * hr + s_ref[pl.ds(r0, SUBLANES), im]
                s_ref[pl.ds(r0, SUBLANES), re] = nr
                s_ref[pl.ds(r0, SUBLANES), im] = ni
                return nr, ni
        else:
            a2r, a2i = acon_ref[2, :, re], acon_ref[3, :, re]
            alr, ali = acon_ref[4, :, re], acon_ref[5, :, re]

            def step(k, carry):
                hr, hi = carry
                r0 = pl.multiple_of(k * SUBLANES, SUBLANES)
                xr = s_ref[pl.ds(r0, SUBLANES), re]
                xi = s_ref[pl.ds(r0, SUBLANES), im]
                sr = pltpu.roll(xr, SUBLANES // 2, 0)
                si = pltpu.roll(xi, SUBLANES // 2, 0)
                nr = (a2r * hr - a2i * hi) + (alr * sr - ali * si) + xr
                ni = (a2r * hi + a2i * hr) + (alr * si + ali * sr) + xi
                s_ref[pl.ds(r0, SUBLANES), re] = nr
                s_ref[pl.ds(r0, SUBLANES), im] = ni
                return (jnp.where(upper, pltpu.roll(nr, SUBLANES // 2, 0), nr),
                        jnp.where(upper, pltpu.roll(ni, SUBLANES // 2, 0), ni))

        hr, hi = lax.fori_loop(0, rows // SUBLANES, step, (h_ref[0, :, re], h_ref[1, :, re]), unroll=2)
        h_ref[0, :, re] = hr
        h_ref[1, :, re] = hi

    @pl.when(i == pl.num_programs(0) - 1)
    def _():
        st_ref[...] = h_ref[...]

    y = _dot(s_ref[...].astype(BF16), cbd_ref[...]) + d_ref[...] * u
    y = 0.5 * y * (1.0 + jnp.tanh(math.sqrt(2.0 / math.pi) * (y + 0.044715 * (y * y * y))))
    y = y * _sigmoid(_dot(y.astype(BF16), wglu_ref[...]) + bglu_ref[...])
    y_ref[...] = _rms(y, gs_ref[...]).astype(BF16)


def _s5_call(u_tb, h0, bbd, cbd, acon, s5_d, wglu, b_glu, g_s5, nb, tt, name):
    n = u_tb.shape[0]
    rows = tt * nb
    blk = pl.BlockSpec((rows, D_S5), lambda i: (i, 0))
    vec = lambda a: a.reshape(1, D_S5)
    return pl.pallas_call(
        functools.partial(_s5_kernel, nb=nb),
        grid=(n // rows,),
        in_specs=[blk, _resident(h0.shape), _resident(bbd.shape), _resident(cbd.shape),
                  _resident(acon.shape), _resident((1, D_S5)), _resident(wglu.shape),
                  _resident((1, D_S5)), _resident((1, D_S5))],
        out_specs=[blk, pl.BlockSpec(h0.shape, lambda i: (0, 0, 0))],
        out_shape=[jax.ShapeDtypeStruct((n, D_S5), BF16), jax.ShapeDtypeStruct(h0.shape, F32)],
        scratch_shapes=[pltpu.VMEM((rows, 2 * N_STATE), F32), pltpu.VMEM(h0.shape, F32)],
        compiler_params=_params(),
        name=name,
    )(u_tb, h0, bbd, cbd, acon, vec(s5_d), wglu, vec(b_glu), vec(g_s5))


FFN_TM = 512
S5_TT = 128


def kernel(x_prompt, x_sample, c_prompt, c_sample, cache_fox_k, cache_fox_v, cache_fox_logf, state_s5_re, state_s5_im, w_ada, b_ada, g_ffn1, w_up1, w_down1, g_mix, w_in, b_fgate, lam_re, lam_im, log_dt, s5_b_re, s5_b_im, s5_c_re, s5_c_im, s5_d, w_glu, b_glu, g_s5_out, g_fox_out, w_out, g_ffn2, w_up2, w_down2, g_final):
    assert w_ada.shape[0] == 1, "single-layer trunk"
    bp, seq, _ = x_prompt.shape
    bs, nq, _ = x_sample.shape
    past = cache_fox_k.shape[2]
    n_p, n_s = bp * seq, bs * nq

    wup1, wdn1 = w_up1[0].astype(BF16), w_down1[0].astype(BF16)
    wup2, wdn2 = w_up2[0].astype(BF16), w_down2[0].astype(BF16)
    w_inb = w_in[0].astype(BF16)
    w_nat_s = w_inb[:, :D_S5 + 3 * D_FOX]
    w_nat_p = jnp.concatenate([w_inb[:, :D_S5], w_inb[:, D_S5 + D_FOX:D_S5 + 2 * D_FOX]], axis=1)
    w_t = jnp.pad(w_inb[:, D_S5:].T, ((0, F_ROWS - FOX_HEADS), (0, 0)))
    b_f = jnp.pad(b_fgate[0], (0, F_ROWS - FOX_HEADS)).reshape(F_ROWS, 1)
    wo = w_out[0].astype(BF16)
    wglu = w_glu[0].astype(BF16)

    pad_rows = -(bp + bs) % SUBLANES
    c_all = jnp.concatenate([c_prompt, c_sample, jnp.zeros((pad_rows, D_MODEL), F32)], axis=0)
    mod = _ada_call(c_all, w_ada[0], b_ada[0])
    mod_p = mod[:bp].reshape(bp, 1, N_MOD * D_MODEL)
    mod_s = jnp.repeat(mod[bp:bp + bs], nq, axis=0)

    bbd, cbd, acon = _s5_prep_call(lam_re[0], lam_im[0], log_dt[0], s5_b_re[0], s5_b_im[0],
                                   s5_c_re[0], s5_c_im[0])

    tm = FFN_TM
    tpb = seq // tm
    xp = x_prompt.reshape(n_p, D_MODEL)
    x1 = _ffn_call(xp, mod_p, 0, False, tm, tpb, g_ffn1[0], wup1, wdn1, name="ffn1_prompt")
    u_tb, qtb, kb, vtb, kt_p, vt_p, lft_p = _inproj_call(
        x1, mod_p, False, tm, tpb, g_mix[0], w_nat_p, w_t, b_f, True, "inproj_prompt")
    f_row = _cumsum_call(lft_p, "cumsum_prompt")
    f_col = jnp.transpose(f_row, (0, 2, 1))
    o_p = _attn_call(qtb, kb, vtb, f_col, f_row, g_fox_out[0], bp, seq)
    h0_p = jnp.zeros((2, SUBLANES, N_STATE), F32)
    ys_tb, st_p = _s5_call(u_tb.reshape(seq * bp, D_S5), h0_p, bbd, cbd, acon, s5_d[0], wglu, b_glu[0],
                           g_s5_out[0], bp, S5_TT, "s5_prompt")
    ys_spec = pl.BlockSpec((tm, D_S5), lambda i: (i % tpb, i // tpb))
    y_p = _ffn_call(x1, mod_p, 6, False, tm, tpb, g_ffn2[0], wup2, wdn2,
                       premix=(ys_tb.reshape(seq, bp * D_S5), ys_spec, o_p.reshape(n_p, D_FOX), wo, 5),
                       final_g=g_final, name="ffn2_prompt")

    xs = x_sample.reshape(n_s, D_MODEL)
    xs1 = _ffn_call(xs, mod_s, 0, True, n_s, 1, g_ffn1[0], wup1, wdn1, name="ffn1_sample")
    u_s, qb_s, ktb_s, vtb_s, k_s, v_s, lft_s = _inproj_call(
        xs1, mod_s, True, n_s, 1, g_mix[0], w_nat_s, w_t, b_f, False, "inproj_sample")
    per_batch_t = lambda a: jnp.transpose(a.reshape(a.shape[0], bs, nq), (1, 0, 2))
    lft_s = per_batch_t(lft_s)
    f_pad = -(past + nq) % CUM_CHUNK
    lft_all = jnp.concatenate([jnp.transpose(cache_fox_logf[0], (0, 2, 1)), lft_s,
                               jnp.zeros((bs, FOX_HEADS, f_pad), F32)], axis=2)
    f_row_s = _cumsum_call(lft_all, "cumsum_sample")
    fq_col = f_row_s[:, :, past:past + nq].reshape(bs, FOX_HEADS * nq, 1)
    feature_major = lambda c: jnp.transpose(c, (0, 2, 3, 1)).reshape(bs, D_FOX, past)
    o_s = _attn_sample_call(qb_s, per_batch_t(ktb_s), per_batch_t(vtb_s), feature_major(cache_fox_k[0]),
                            feature_major(cache_fox_v[0]), fq_col, f_row_s, g_fox_out[0], bs, nq, past)
    u_s_tb = jnp.transpose(u_s.reshape(bs, nq, D_S5), (1, 0, 2)).reshape(n_s, D_S5)
    h0_s = jnp.stack([state_s5_re[0].reshape(bs, N_STATE), state_s5_im[0].reshape(bs, N_STATE)])
    ys_s_tb, st_s = _s5_call(u_s_tb, h0_s, bbd, cbd, acon, s5_d[0], wglu, b_glu[0], g_s5_out[0],
                             bs, nq, "s5_sample")
    ys_s = jnp.transpose(ys_s_tb.reshape(nq, bs, D_S5), (1, 0, 2)).reshape(n_s, D_S5)
    y_s = _ffn_call(xs1, mod_s, 6, True, n_s, 1, g_ffn2[0], wup2, wdn2,
                       premix=(ys_s, pl.BlockSpec((n_s, D_S5), lambda i: (i, 0)), o_s.reshape(n_s, D_FOX), wo, 5),
                       final_g=g_final, name="ffn2_sample")

    heads = lambda a, b, t: a.reshape(1, b, t, FOX_HEADS, FOX_HEAD_DIM)
    heads_t = lambda a, b, t: jnp.transpose(a.reshape(1, b, FOX_HEADS, FOX_HEAD_DIM, t), (0, 1, 4, 2, 3))
    gates_t = lambda a: jnp.transpose(a, (0, 2, 1))[None]
    state = lambda a, b: a[:b].reshape(1, b, S5_GROUPS, S5_STATE)
    return (y_p.reshape(bp, seq, D_MODEL), y_s.reshape(bs, nq, D_MODEL),
            heads_t(kt_p, bp, seq), heads_t(vt_p, bp, seq), gates_t(lft_p),
            state(st_p[0], bp), state(st_p[1], bp),
            heads(k_s, bs, nq), heads(v_s, bs, nq), gates_t(lft_s),
            state(st_s[0], bs), state(st_s[1], bs))
```

```python
import functools
import math

import jax
import jax.numpy as jnp
from jax import lax
from jax.experimental import pallas as pl
from jax.experimental.pallas import tpu as pltpu

F32 = jnp.float32
BF16 = jnp.bfloat16

D_MODEL = 1024
D_FF = 2816
D_S5 = 512
D_FOX = 512
FOX_HEADS = 8
FOX_HEAD_DIM = 64
S5_GROUPS = 32
S5_GROUP = 16
S5_STATE = 64
N_STATE = S5_GROUPS * S5_STATE
N_MOD = 9
ATTN_SCALE = FOX_HEAD_DIM ** -0.5
NEG_INF = -1e30
EPS = 1e-6
LANES = 128
SUBLANES = 8
VMEM_LIMIT = 56 * 1024 * 1024


def _params(n_axes=1, vmem=VMEM_LIMIT):
    return pltpu.CompilerParams(dimension_semantics=("arbitrary",) * n_axes, vmem_limit_bytes=vmem)


def _resident(shape):
    nd = len(shape)
    return pl.BlockSpec(shape, lambda *_: (0,) * nd, pipeline_mode=pl.Buffered(1))


def _sigmoid(x):
    return 1.0 / (1.0 + jnp.exp(-x))


def _rms(x, g):
    y = x * lax.rsqrt(jnp.mean(x * x, axis=-1, keepdims=True) + EPS)
    return y * g


def _dot(a, b):
    return jnp.dot(a, b, preferred_element_type=F32)


def _dot_nt(a, b):
    return lax.dot_general(a, b, (((1,), (1,)), ((), ())), preferred_element_type=F32)


def _split3(x):
    hi = x.astype(BF16)
    r = x - hi.astype(F32)
    mid = r.astype(BF16)
    lo = (r - mid.astype(F32)).astype(BF16)
    return hi, mid, lo


def _ada_kernel(c_ref, w_ref, b_ref, o_ref):
    c = c_ref[...]
    s = (c * _sigmoid(c)).astype(BF16)
    o_ref[...] = _dot(s, w_ref[...].astype(BF16)) + b_ref[...]


def _ada_call(c_all, w_ada, b_ada):
    rows = c_all.shape[0]
    n = w_ada.shape[1]
    bn = D_MODEL
    return pl.pallas_call(
        _ada_kernel,
        grid=(n // bn,),
        in_specs=[pl.BlockSpec((rows, D_MODEL), lambda j: (0, 0)),
                  pl.BlockSpec((D_MODEL, bn), lambda j: (0, j)),
                  pl.BlockSpec((1, bn), lambda j: (0, j))],
        out_specs=pl.BlockSpec((rows, bn), lambda j: (0, j)),
        out_shape=jax.ShapeDtypeStruct((rows, n), F32),
        compiler_params=_params(),
        name="ada",
    )(c_all, w_ada, b_ada.reshape(1, n))


def _mod_spec(per_row, tm, tpb, k):
    if per_row:
        return pl.BlockSpec((tm, D_MODEL), lambda i: (i, k))
    return pl.BlockSpec((None, 1, D_MODEL), lambda i: (i // tpb, 0, k))


def _ffn_kernel(*refs, premix, final):
    it = iter(refs)
    x_ref = next(it)
    if premix:
        ys_ref, of_ref, wo_ref, g2_ref = next(it), next(it), next(it), next(it)
    sh_ref, sc_ref, gt_ref, gn_ref, wup_ref, wdn_ref = (next(it) for _ in range(6))
    if final:
        gf_ref = next(it)
    o_ref = next(it)

    x = x_ref[...]
    if premix:
        mixed = _dot(ys_ref[...], wo_ref[:D_S5, :]) + _dot(of_ref[...], wo_ref[D_S5:, :])
        x = x + (1.0 + g2_ref[...]) * mixed
    h = _rms(x, gn_ref[...]) * (1.0 + sc_ref[...]) + sh_ref[...]
    gu = _dot(h.astype(BF16), wup_ref[...])
    g = gu[:, :D_FF]
    u = gu[:, D_FF:]
    a = (g * _sigmoid(g) * u).astype(BF16)
    y = _dot(a, wdn_ref[...])
    xn = x + (0.5 * (1.0 + gt_ref[...])) * y
    o_ref[...] = _rms(xn, gf_ref[...]) if final else xn


def _ffn_call(x, mod, k0, per_row, tm, tpb, gn, wup, wdn, *, premix=None, final_g=None, name):
    n = x.shape[0]
    tok = pl.BlockSpec((tm, D_MODEL), lambda i: (i, 0))
    args, specs = [x], [tok]
    if premix is not None:
        ys, ys_spec, of, wo, k2 = premix
        args += [ys, of, wo, mod]
        specs += [ys_spec, pl.BlockSpec((tm, D_FOX), lambda i: (i, 0)), _resident(wo.shape),
                  _mod_spec(per_row, tm, tpb, k2)]
    args += [mod, mod, mod, gn.reshape(1, D_MODEL), wup, wdn]
    specs += [_mod_spec(per_row, tm, tpb, k0), _mod_spec(per_row, tm, tpb, k0 + 1),
              _mod_spec(per_row, tm, tpb, k0 + 2), _resident((1, D_MODEL)),
              _resident(wup.shape), _resident(wdn.shape)]
    final = final_g is not None
    if final:
        args.append(final_g.reshape(1, D_MODEL))
        specs.append(_resident((1, D_MODEL)))
    return pl.pallas_call(
        functools.partial(_ffn_kernel, premix=premix is not None, final=final),
        grid=(n // tm,),
        in_specs=specs,
        out_specs=tok,
        out_shape=jax.ShapeDtypeStruct((n, D_MODEL), F32),
        compiler_params=_params(),
        name=name,
    )(*args)


F_ROWS = 16


def _inproj_kernel(x_ref, sh_ref, sc_ref, gn_ref, wn_ref, wt_ref, bf_ref, *out_refs, prompt):
    h = (_rms(x_ref[...], gn_ref[...]) * (1.0 + sc_ref[...]) + sh_ref[...]).astype(BF16)
    nat = _dot(h, wn_ref[...])
    pt = _dot_nt(wt_ref[...], h)
    q_t, k_t, v_t = pt[:D_FOX], pt[D_FOX:2 * D_FOX], pt[2 * D_FOX:3 * D_FOX]
    z = pt[3 * D_FOX:] + bf_ref[...]
    lf = (jnp.minimum(z, 0.0) - jnp.log1p(jnp.exp(-jnp.abs(z))))[:FOX_HEADS]
    if prompt:
        u_ref, qtb_ref, kb_ref, vtb_ref, kt_ref, vt_ref, lf_ref = out_refs
        u_ref[...] = nat[:, :D_S5]
        qtb_ref[...] = (q_t * (ATTN_SCALE * LOG2E)).astype(BF16)
        kb_ref[...] = nat[:, D_S5:].astype(BF16)
        vtb_ref[...] = v_t.astype(BF16)
        kt_ref[...] = k_t
        vt_ref[...] = v_t
    else:
        u_ref, qb_ref, ktb_ref, vtb_ref, k_ref, v_ref, lf_ref = out_refs
        u_ref[...] = nat[:, :D_S5]
        qb_ref[...] = (nat[:, D_S5:D_S5 + D_FOX] * ATTN_SCALE).astype(BF16)
        ktb_ref[...] = k_t.astype(BF16)
        vtb_ref[...] = v_t.astype(BF16)
        k_ref[...] = nat[:, D_S5 + D_FOX:D_S5 + 2 * D_FOX]
        v_ref[...] = nat[:, D_S5 + 2 * D_FOX:]
    lf_ref[...] = lf


def _inproj_call(x, mod, per_row, tm, tpb, gn, w_nat, w_t, b_f, prompt, name):
    n = x.shape[0]
    nb = n // (tm * tpb)
    seq = tm * tpb
    sd = jax.ShapeDtypeStruct
    tok = lambda w: pl.BlockSpec((tm, w), lambda i: (i, 0))
    if prompt:
        feat = lambda r: pl.BlockSpec((None, r, tm), lambda i: (i // tpb, 0, i % tpb))
        fshape = lambda r, dt: sd((nb, r, seq), dt)
        out_specs = [pl.BlockSpec((tm, D_S5), lambda i: (i % tpb, i // tpb)), feat(D_FOX), tok(D_FOX),
                     feat(D_FOX), feat(D_FOX), feat(D_FOX), feat(FOX_HEADS)]
        out_shape = [sd((seq, nb * D_S5), F32), fshape(D_FOX, BF16), sd((n, D_FOX), BF16),
                     fshape(D_FOX, BF16), fshape(D_FOX, F32), fshape(D_FOX, F32), fshape(FOX_HEADS, F32)]
    else:
        feat = lambda r: pl.BlockSpec((r, tm), lambda i: (0, i))
        out_specs = [tok(D_S5), tok(D_FOX), feat(D_FOX), feat(D_FOX), tok(D_FOX), tok(D_FOX), feat(FOX_HEADS)]
        out_shape = [sd((n, D_S5), F32), sd((n, D_FOX), BF16), sd((D_FOX, n), BF16), sd((D_FOX, n), BF16),
                     sd((n, D_FOX), F32), sd((n, D_FOX), F32), sd((FOX_HEADS, n), F32)]
    return pl.pallas_call(
        functools.partial(_inproj_kernel, prompt=prompt),
        grid=(n // tm,),
        in_specs=[tok(D_MODEL), _mod_spec(per_row, tm, tpb, 3), _mod_spec(per_row, tm, tpb, 4),
                  _resident((1, D_MODEL)), _resident(w_nat.shape), _resident(w_t.shape),
                  _resident((F_ROWS, 1))],
        out_specs=out_specs,
        out_shape=out_shape,
        compiler_params=_params(),
        name=name,
    )(x, mod, mod, gn.reshape(1, D_MODEL), w_nat, w_t, b_f)


CUM_CHUNK = 512


def _cumsum_kernel(x_ref, o_ref):
    t = x_ref.shape[-1]
    row = lax.broadcasted_iota(jnp.int32, (CUM_CHUNK, CUM_CHUNK), 0)
    col = lax.broadcasted_iota(jnp.int32, (CUM_CHUNK, CUM_CHUNK), 1)
    tri = jnp.where(row <= col, 1.0, 0.0).astype(BF16)
    carry = jnp.zeros((x_ref.shape[0], 1), F32)
    for c in range(t // CUM_CHUNK):
        sl = slice(c * CUM_CHUNK, (c + 1) * CUM_CHUNK)
        hi, mid, lo = _split3(x_ref[:, sl])
        cum = (_dot(hi, tri) + _dot(mid, tri)) + _dot(lo, tri) + carry
        o_ref[:, sl] = cum
        carry = cum[:, CUM_CHUNK - 1:CUM_CHUNK]


def _cumsum_call(x_t, name):
    b, h, t = x_t.shape
    spec = pl.BlockSpec((None, h, t), lambda i: (i, 0, 0))
    return pl.pallas_call(
        _cumsum_kernel, grid=(b,), in_specs=[spec], out_specs=spec,
        out_shape=jax.ShapeDtypeStruct((b, h, t), F32), compiler_params=_params(), name=name,
    )(x_t)


ATTN_TQ = 512
ONES_ROWS = 16
LOG2E = math.log2(math.e)


def _attn_kernel(qt_ref, k_ref, vt_ref, fq_ref, fk_ref, g_ref, o_ref,
                 ka_ref, qa_ref, m_ref, acc_ref, r0_ref, r1_ref, s0_ref, s1_ref, ot_scr, *, tq):
    qi = pl.program_id(1)
    pair = 2 * FOX_HEAD_DIM

    @pl.when(qi == 0)
    def _():
        hi, mid, lo = _split3(fk_ref[...] * (-LOG2E))
        hd = lax.broadcasted_iota(jnp.int32, (FOX_HEADS, D_FOX), 0)
        ln = lax.broadcasted_iota(jnp.int32, (FOX_HEADS, D_FOX), 1)
        base = (hd // 2) * pair + 3 * (hd % 2)
        place = lambda t: jnp.where(ln == base + t, 1.0, 0.0).astype(BF16)
        ka_ref[...] = (_dot(hi, place(0)) + _dot(mid, place(1)) + _dot(lo, place(2))).astype(BF16)

    frow = lax.broadcasted_iota(jnp.int32, (pair, tq), 0)
    head0 = frow < FOX_HEAD_DIM
    key = lax.broadcasted_iota(jnp.int32, (tq, tq), 0)
    qry = lax.broadcasted_iota(jnp.int32, (tq, tq), 1)
    causal = key <= qry
    ones_rows = jnp.ones((ONES_ROWS, tq), BF16)

    for hp in range(FOX_HEADS // 2):
        fs = slice(hp * pair, (hp + 1) * pair)
        q2 = qt_ref[fs, :]
        zero = jnp.zeros_like(q2)
        for e in range(2):
            qa_ref[e, :pair, :] = jnp.where(head0 == (e == 0), q2, zero)
            qa_ref[e, pair:, :] = jnp.where((frow >= 3 * e) & (frow < 3 * e + 3), 1.0, 0.0).astype(BF16)
        fqs = tuple(fq_ref[2 * hp + e:2 * hp + e + 1, :] * LOG2E for e in range(2))
        m_ref[...] = jnp.full(m_ref.shape, NEG_INF, F32)
        acc_ref[...] = jnp.zeros(acc_ref.shape, F32)

        def scores(j, masked, r_ref, st_ref):
            off = pl.multiple_of(j * tq, tq)
            k2 = jnp.concatenate([k_ref[pl.ds(off, tq), fs], ka_ref[pl.ds(off, tq), fs]], axis=1)
            for e in range(2):
                r = _dot(k2, qa_ref[e])
                if masked:
                    r = jnp.where(causal, r, NEG_INF)
                m_prev = m_ref[e]
                m_new = jnp.maximum(m_prev, jnp.max(r, axis=0, keepdims=True) + fqs[e])
                st_ref[e, 0] = jnp.exp2(m_prev - m_new)
                st_ref[e, 1] = fqs[e] - m_new
                m_ref[e] = m_new
                r_ref[e] = r

        def accumulate(j, r_ref, st_ref):
            off = pl.multiple_of(j * tq, tq)
            for e in range(2):
                vt = vt_ref[hp * pair + e * FOX_HEAD_DIM:hp * pair + (e + 1) * FOX_HEAD_DIM, pl.ds(off, tq)]
                p = jnp.exp2(r_ref[e] + st_ref[e, 1]).astype(BF16)
                acc_ref[e] = st_ref[e, 0] * acc_ref[e] + _dot(jnp.concatenate([vt, ones_rows], axis=0), p)

        scores(qi, True, r0_ref, s0_ref)

        def body(i, c):
            j = qi - 2 * i
            scores(j - 1, False, r1_ref, s1_ref)
            accumulate(j, r0_ref, s0_ref)
            scores(j - 2, False, r0_ref, s0_ref)
            accumulate(j - 1, r1_ref, s1_ref)
            return c

        lax.fori_loop(0, qi // 2, body, 0)

        @pl.when(qi % 2 == 1)
        def _():
            scores(0, False, r1_ref, s1_ref)
            accumulate(1, r0_ref, s0_ref)
            accumulate(0, r1_ref, s1_ref)

        @pl.when(qi % 2 == 0)
        def _():
            accumulate(0, r0_ref, s0_ref)

        for e in range(2):
            denom = acc_ref[e, FOX_HEAD_DIM:FOX_HEAD_DIM + 1, :]
            ot_scr[hp * pair + e * FOX_HEAD_DIM:hp * pair + (e + 1) * FOX_HEAD_DIM, :] = (
                acc_ref[e, :FOX_HEAD_DIM, :] / denom)

    ot = ot_scr[...]
    y = ot * lax.rsqrt(jnp.mean(ot * ot, axis=0, keepdims=True) + EPS) * g_ref[...]
    o_ref[...] = y.T.astype(BF16)


def _attn_call(qtb, kb, vtb, f_col, f_row, g_fox, batch, seq):
    tq = ATTN_TQ
    nq = seq // tq
    whole = lambda r, c: pl.BlockSpec((None, r, c), lambda b, i: (b, 0, 0), pipeline_mode=pl.Buffered(1))
    return pl.pallas_call(
        functools.partial(_attn_kernel, tq=tq),
        grid=(batch, nq),
        in_specs=[pl.BlockSpec((None, D_FOX, tq), lambda b, i: (b, 0, i)),
                  whole(seq, D_FOX), whole(D_FOX, seq),
                  pl.BlockSpec((None, FOX_HEADS, tq), lambda b, i: (b, 0, i)),
                  whole(seq, FOX_HEADS),
                  pl.BlockSpec((D_FOX, 1), lambda b, i: (0, 0))],
        out_specs=pl.BlockSpec((None, tq, D_FOX), lambda b, i: (b, i, 0)),
        out_shape=jax.ShapeDtypeStruct((batch, seq, D_FOX), BF16),
        scratch_shapes=[pltpu.VMEM((seq, D_FOX), BF16), pltpu.VMEM((2, 4 * FOX_HEAD_DIM, tq), BF16),
                        pltpu.VMEM((2, 1, tq), F32), pltpu.VMEM((2, FOX_HEAD_DIM + ONES_ROWS, tq), F32),
                        pltpu.VMEM((2, tq, tq), F32), pltpu.VMEM((2, tq, tq), F32),
                        pltpu.VMEM((2, 2, 1, tq), F32), pltpu.VMEM((2, 2, 1, tq), F32),
                        pltpu.VMEM((D_FOX, tq), F32)],
        compiler_params=_params(2),
        name="fox_attn_prompt",
    )(qtb, kb.reshape(batch, seq, D_FOX), vtb, f_row, f_col, g_fox.reshape(D_FOX, 1))


SAMPLE_KC = 1024


def _attn_sample_kernel(q_ref, kn_ref, vn_ref, ck_ref, cv_ref, fq_ref, fk_ref, g_ref, o_ref,
                        qbd_ref, m_ref, l_ref, acc_ref, *, nq, past):
    c = pl.program_id(1)
    nc = pl.num_programs(1)
    rows = FOX_HEADS * nq
    rblk = lax.broadcasted_iota(jnp.int32, (rows, D_FOX), 0) // nq
    cblk = lax.broadcasted_iota(jnp.int32, (rows, D_FOX), 1) // FOX_HEAD_DIM
    own_head = rblk == cblk

    @pl.when(c == 0)
    def _():
        qt = jnp.concatenate([q_ref[...]] * FOX_HEADS, axis=0)
        qbd_ref[...] = jnp.where(own_head, qt, jnp.zeros_like(qt))
        m_ref[...] = jnp.full(m_ref.shape, NEG_INF, F32)
        l_ref[...] = jnp.zeros(l_ref.shape, F32)
        acc_ref[...] = jnp.zeros(acc_ref.shape, F32)

    def expand_heads(f):
        n = f.shape[1]
        return jnp.concatenate([jnp.broadcast_to(f[h:h + 1, :], (nq, n)) for h in range(FOX_HEADS)], axis=0)

    def update(kt, vt, fk, mask):
        s = _dot(qbd_ref[...], kt) + fq_ref[...] - expand_heads(fk)
        if mask is not None:
            s = jnp.where(mask, s, NEG_INF)
        m_prev = m_ref[...]
        m_new = jnp.maximum(m_prev, jnp.max(s, axis=1, keepdims=True))
        alpha = jnp.exp(m_prev - m_new)
        p = jnp.exp(s - m_new)
        l_ref[...] = alpha * l_ref[...] + jnp.sum(p, axis=1, keepdims=True)
        acc_ref[...] = alpha * acc_ref[...] + _dot_nt(p.astype(BF16), vt)
        m_ref[...] = m_new

    off = pl.multiple_of(c * SAMPLE_KC, SAMPLE_KC)
    update(ck_ref[...].astype(BF16), cv_ref[...].astype(BF16), fk_ref[:, pl.ds(off, SAMPLE_KC)], None)

    @pl.when(c == nc - 1)
    def _():
        qpos = lax.broadcasted_iota(jnp.int32, (rows, nq), 0) % nq
        kpos = lax.broadcasted_iota(jnp.int32, (rows, nq), 1)
        update(kn_ref[...], vn_ref[...], fk_ref[:, past:past + nq], kpos <= qpos)
        o_full = jnp.where(own_head, acc_ref[...] / l_ref[...], 0.0)
        o = o_full[0:nq, :]
        for h in range(1, FOX_HEADS):
            o = o + o_full[h * nq:(h + 1) * nq, :]
        o_ref[...] = _rms(o, g_ref[...]).astype(BF16)


def _attn_sample_call(qb, ktb, vtb, cache_kt, cache_vt, fq_col, f_row, g_fox, batch, nq, past):
    rows = FOX_HEADS * nq
    nc = past // SAMPLE_KC
    per_b = lambda w: pl.BlockSpec((None, nq, w), lambda b, c: (b, 0, 0))
    new_t = pl.BlockSpec((None, D_FOX, nq), lambda b, c: (b, 0, 0))
    cache = pl.BlockSpec((None, D_FOX, SAMPLE_KC), lambda b, c: (b, 0, c))
    return pl.pallas_call(
        functools.partial(_attn_sample_kernel, nq=nq, past=past),
        grid=(batch, nc),
        in_specs=[per_b(D_FOX), new_t, new_t, cache, cache,
                  pl.BlockSpec((None, rows, 1), lambda b, c: (b, 0, 0)),
                  pl.BlockSpec((None, FOX_HEADS, f_row.shape[-1]), lambda b, c: (b, 0, 0)),
                  pl.BlockSpec((1, D_FOX), lambda b, c: (0, 0))],
        out_specs=per_b(D_FOX),
        out_shape=jax.ShapeDtypeStruct((batch, nq, D_FOX), BF16),
        scratch_shapes=[pltpu.VMEM((rows, D_FOX), BF16), pltpu.VMEM((rows, 1), F32),
                        pltpu.VMEM((rows, 1), F32), pltpu.VMEM((rows, D_FOX), F32)],
        compiler_params=_params(2),
        name="fox_attn_sample",
    )(qb.reshape(batch, nq, D_FOX), ktb, vtb, cache_kt, cache_vt, fq_col, f_row, g_fox.reshape(1, D_FOX))


def _s5_prep_kernel(lre_ref, lim_ref, ldt_ref, bre_ref, bim_ref, cre_ref, cim_ref,
                    lre_row_ref, lim_row_ref, ldt_row_ref, bbd_ref, cbd_ref, acon_ref):
    def a_bar(lre, lim, ldt):
        dt = jnp.exp(ldt)
        mag = jnp.exp(lre * dt)
        return mag * jnp.cos(lim * dt), mag * jnp.sin(lim * dt)

    lre, lim = lre_ref[...], lim_ref[...]
    ar, ai = a_bar(lre, lim, ldt_ref[...])
    den = lre * lre + lim * lim
    cr = ((ar - 1.0) * lre + ai * lim) / den
    ci = (ai * lre - (ar - 1.0) * lim) / den
    bre, bim = bre_ref[...], bim_ref[...]
    b_r = (cr * bre - ci * bim).astype(BF16)
    b_i = (cr * bim + ci * bre).astype(BF16)
    srow = lax.broadcasted_iota(jnp.int32, (S5_STATE, N_STATE), 0)
    scol = lax.broadcasted_iota(jnp.int32, (S5_STATE, N_STATE), 1)
    rep_state = jnp.where(scol % S5_STATE == srow, 1.0, 0.0).astype(BF16)
    grow = lax.broadcasted_iota(jnp.int32, (D_S5, N_STATE), 0) // S5_GROUP
    gcol = lax.broadcasted_iota(jnp.int32, (D_S5, N_STATE), 1) // S5_STATE
    own = grow == gcol
    bbd_ref[:, :N_STATE] = jnp.where(own, _dot(b_r, rep_state), 0.0).astype(BF16)
    bbd_ref[:, N_STATE:] = jnp.where(own, _dot(b_i, rep_state), 0.0).astype(BF16)

    crow = lax.broadcasted_iota(jnp.int32, (S5_GROUP, D_S5), 0)
    ccol = lax.broadcasted_iota(jnp.int32, (S5_GROUP, D_S5), 1)
    rep_chan = jnp.where(ccol % S5_GROUP == crow, 1.0, 0.0).astype(BF16)
    grow2 = lax.broadcasted_iota(jnp.int32, (N_STATE, D_S5), 0) // S5_STATE
    gcol2 = lax.broadcasted_iota(jnp.int32, (N_STATE, D_S5), 1) // S5_GROUP
    own2 = grow2 == gcol2
    cbd_ref[:N_STATE, :] = jnp.where(own2, _dot(cre_ref[...].astype(BF16), rep_chan), 0.0).astype(BF16)
    cbd_ref[N_STATE:, :] = jnp.where(own2, -_dot(cim_ref[...].astype(BF16), rep_chan), 0.0).astype(BF16)

    ar1, ai1 = a_bar(lre_row_ref[...], lim_row_ref[...], ldt_row_ref[...])
    ar2 = ar1 * ar1 - ai1 * ai1
    ai2 = 2.0 * ar1 * ai1
    upper = lax.broadcasted_iota(jnp.int32, (SUBLANES, N_STATE), 0) < SUBLANES // 2
    bc = lambda v: jnp.broadcast_to(v, (SUBLANES, N_STATE))
    acon_ref[0] = bc(ar1)
    acon_ref[1] = bc(ai1)
    acon_ref[2] = jnp.where(upper, bc(ar1), bc(ar2))
    acon_ref[3] = jnp.where(upper, bc(ai1), bc(ai2))
    acon_ref[4] = jnp.where(upper, 0.0, bc(ar1))
    acon_ref[5] = jnp.where(upper, 0.0, bc(ai1))


def _s5_prep_call(lam_re, lam_im, log_dt, b_re, b_im, c_re, c_im):
    rep_rows = lambda a: jnp.repeat(a, S5_GROUP, axis=0)
    ldt_gp = jnp.broadcast_to(log_dt[:, None], (S5_GROUPS, S5_STATE))
    to_cp = lambda b: jnp.transpose(b, (0, 2, 1)).reshape(D_S5, S5_STATE)
    to_pc = lambda c: jnp.transpose(c, (0, 2, 1)).reshape(N_STATE, S5_GROUP)
    row = lambda a: a.reshape(1, N_STATE)
    args = (rep_rows(lam_re), rep_rows(lam_im), rep_rows(ldt_gp), to_cp(b_re), to_cp(b_im),
            to_pc(c_re), to_pc(c_im), row(lam_re), row(lam_im), row(ldt_gp))
    sd = jax.ShapeDtypeStruct
    return pl.pallas_call(
        _s5_prep_kernel,
        out_shape=[sd((D_S5, 2 * N_STATE), BF16), sd((2 * N_STATE, D_S5), BF16),
                   sd((6, SUBLANES, N_STATE), F32)],
        compiler_params=pltpu.CompilerParams(vmem_limit_bytes=VMEM_LIMIT),
        name="s5_prep",
    )(*args)


SCAN_COLS = 512


def _s5_kernel(u_ref, h0_ref, bbd_ref, cbd_ref, acon_ref, d_ref, wglu_ref, bglu_ref, gs_ref,
               y_ref, st_ref, s_ref, h_ref, *, nb):
    i = pl.program_id(0)
    rows = u_ref.shape[0]

    @pl.when(i == 0)
    def _():
        h_ref[...] = h0_ref[...]

    u = u_ref[...]
    s_ref[...] = _dot(u.astype(BF16), bbd_ref[...])

    upper = lax.broadcasted_iota(jnp.int32, (SUBLANES, SCAN_COLS), 0) < SUBLANES // 2
    for c in range(N_STATE // SCAN_COLS):
        re = slice(c * SCAN_COLS, (c + 1) * SCAN_COLS)
        im = slice(N_STATE + c * SCAN_COLS, N_STATE + (c + 1) * SCAN_COLS)
        if nb == SUBLANES:
            ar, ai = acon_ref[0, :, re], acon_ref[1, :, re]

            def step(k, carry):
                hr, hi = carry
                r0 = pl.multiple_of(k * SUBLANES, SUBLANES)
                nr = ar * hr - ai * hi + s_ref[pl.ds(r0, SUBLANES), re]
                ni = ar * hi + ai * hr + s_ref[pl.ds(r0, SUBLANES), im]
                s_ref[pl.ds(r0, SUBLANES), re] = nr
                s_ref[pl.ds(r0, SUBLANES), im] = ni
                return nr, ni
        else:
            a2r, a2i = acon_ref[2, :, re], acon_ref[3, :, re]
            alr, ali = acon_ref[4, :, re], acon_ref[5, :, re]

            def step(k, carry):
                hr, hi = carry
                r0 = pl.multiple_of(k * SUBLANES, SUBLANES)
                xr = s_ref[pl.ds(r0, SUBLANES), re]
                xi = s_ref[pl.ds(r0, SUBLANES), im]
                sr = pltpu.roll(xr, SUBLANES // 2, 0)
                si = pltpu.roll(xi, SUBLANES // 2, 0)
                nr = (a2r * hr - a2i * hi) + (alr * sr - ali * si) + xr
                ni = (a2r * hi + a2i * hr) + (alr * si + ali * sr) + xi
                s_ref[pl.ds(r0, SUBLANES), re] = nr
                s_ref[pl.ds(r0, SUBLANES), im] = ni
                return (jnp.where(upper, pltpu.roll(nr, SUBLANES // 2, 0), nr),
                        jnp.where(upper, pltpu.roll(ni, SUBLANES // 2, 0), ni))

        hr, hi = lax.fori_loop(0, rows // SUBLANES, step, (h_ref[0, :, re], h_ref[1, :, re]), unroll=2)
        h_ref[0, :, re] = hr
        h_ref[1, :, re] = hi

    @pl.when(i == pl.num_programs(0) - 1)
    def _():
        st_ref[...] = h_ref[...]

    y = _dot(s_ref[...].astype(BF16), cbd_ref[...]) + d_ref[...] * u
    y = 0.5 * y * (1.0 + jnp.tanh(math.sqrt(2.0 / math.pi) * (y + 0.044715 * (y * y * y))))
    y = y * _sigmoid(_dot(y.astype(BF16), wglu_ref[...]) + bglu_ref[...])
    y_ref[...] = _rms(y, gs_ref[...]).astype(BF16)


def _s5_call(u_tb, h0, bbd, cbd, acon, s5_d, wglu, b_glu, g_s5, nb, tt, name):
    n = u_tb.shape[0]
    rows = tt * nb
    blk = pl.BlockSpec((rows, D_S5), lambda i: (i, 0))
    vec = lambda a: a.reshape(1, D_S5)
    return pl.pallas_call(
        functools.partial(_s5_kernel, nb=nb),
        grid=(n // rows,),
        in_specs=[blk, _resident(h0.shape), _resident(bbd.shape), _resident(cbd.shape),
                  _resident(acon.shape), _resident((1, D_S5)), _resident(wglu.shape),
                  _resident((1, D_S5)), _resident((1, D_S5))],
        out_specs=[blk, pl.BlockSpec(h0.shape, lambda i: (0, 0, 0))],
        out_shape=[jax.ShapeDtypeStruct((n, D_S5), BF16), jax.ShapeDtypeStruct(h0.shape, F32)],
        scratch_shapes=[pltpu.VMEM((rows, 2 * N_STATE), F32), pltpu.VMEM(h0.shape, F32)],
        compiler_params=_params(),
        name=name,
    )(u_tb, h0, bbd, cbd, acon, vec(s5_d), wglu, vec(b_glu), vec(g_s5))


FFN_TM = 512
S5_TT = 128


def kernel(x_prompt, x_sample, c_prompt, c_sample, cache_fox_k, cache_fox_v, cache_fox_logf, state_s5_re, state_s5_im, w_ada, b_ada, g_ffn1, w_up1, w_down1, g_mix, w_in, b_fgate, lam_re, lam_im, log_dt, s5_b_re, s5_b_im, s5_c_re, s5_c_im, s5_d, w_glu, b_glu, g_s5_out, g_fox_out, w_out, g_ffn2, w_up2, w_down2, g_final):
    assert w_ada.shape[0] == 1, "single-layer trunk"
    bp, seq, _ = x_prompt.shape
    bs, nq, _ = x_sample.shape
    past = cache_fox_k.shape[2]
    n_p, n_s = bp * seq, bs * nq

    wup1, wdn1 = w_up1[0].astype(BF16), w_down1[0].astype(BF16)
    wup2, wdn2 = w_up2[0].astype(BF16), w_down2[0].astype(BF16)
    w_inb = w_in[0].astype(BF16)
    w_nat_s = w_inb[:, :D_S5 + 3 * D_FOX]
    w_nat_p = jnp.concatenate([w_inb[:, :D_S5], w_inb[:, D_S5 + D_FOX:D_S5 + 2 * D_FOX]], axis=1)
    w_t = jnp.pad(w_inb[:, D_S5:].T, ((0, F_ROWS - FOX_HEADS), (0, 0)))
    b_f = jnp.pad(b_fgate[0], (0, F_ROWS - FOX_HEADS)).reshape(F_ROWS, 1)
    wo = w_out[0].astype(BF16)
    wglu = w_glu[0].astype(BF16)

    pad_rows = -(bp + bs) % SUBLANES
    c_all = jnp.concatenate([c_prompt, c_sample, jnp.zeros((pad_rows, D_MODEL), F32)], axis=0)
    mod = _ada_call(c_all, w_ada[0], b_ada[0])
    mod_p = mod[:bp].reshape(bp, 1, N_MOD * D_MODEL)
    mod_s = jnp.repeat(mod[bp:bp + bs], nq, axis=0)

    bbd, cbd, acon = _s5_prep_call(lam_re[0], lam_im[0], log_dt[0], s5_b_re[0], s5_b_im[0],
                                   s5_c_re[0], s5_c_im[0])

    tm = FFN_TM
    tpb = seq // tm
    xp = x_prompt.reshape(n_p, D_MODEL)
    x1 = _ffn_call(xp, mod_p, 0, False, tm, tpb, g_ffn1[0], wup1, wdn1, name="ffn1_prompt")
    u_tb, qtb, kb, vtb, kt_p, vt_p, lft_p = _inproj_call(
        x1, mod_p, False, tm, tpb, g_mix[0], w_nat_p, w_t, b_f, True, "inproj_prompt")
    f_row = _cumsum_call(lft_p, "cumsum_prompt")
    f_col = jnp.transpose(f_row, (0, 2, 1))
    o_p = _attn_call(qtb, kb, vtb, f_col, f_row, g_fox_out[0], bp, seq)
    h0_p = jnp.zeros((2, SUBLANES, N_STATE), F32)
    ys_tb, st_p = _s5_call(u_tb.reshape(seq * bp, D_S5), h0_p, bbd, cbd, acon, s5_d[0], wglu, b_glu[0],
                           g_s5_out[0], bp, S5_TT, "s5_prompt")
    ys_spec = pl.BlockSpec((tm, D_S5), lambda i: (i % tpb, i // tpb))
    y_p = _ffn_call(x1, mod_p, 6, False, tm, tpb, g_ffn2[0], wup2, wdn2,
                       premix=(ys_tb.reshape(seq, bp * D_S5), ys_spec, o_p.reshape(n_p, D_FOX), wo, 5),
                       final_g=g_final, name="ffn2_prompt")

    xs = x_sample.reshape(n_s, D_MODEL)
    xs1 = _ffn_call(xs, mod_s, 0, True, n_s, 1, g_ffn1[0], wup1, wdn1, name="ffn1_sample")
    u_s, qb_s, ktb_s, vtb_s, k_s, v_s, lft_s = _inproj_call(
        xs1, mod_s, True, n_s, 1, g_mix[0], w_nat_s, w_t, b_f, False, "inproj_sample")
    per_batch_t = lambda a: jnp.transpose(a.reshape(a.shape[0], bs, nq), (1, 0, 2))
    lft_s = per_batch_t(lft_s)
    f_pad = -(past + nq) % CUM_CHUNK
    lft_all = jnp.concatenate([jnp.transpose(cache_fox_logf[0], (0, 2, 1)), lft_s,
                               jnp.zeros((bs, FOX_HEADS, f_pad), F32)], axis=2)
    f_row_s = _cumsum_call(lft_all, "cumsum_sample")
    fq_col = f_row_s[:, :, past:past + nq].reshape(bs, FOX_HEADS * nq, 1)
    feature_major = lambda c: jnp.transpose(c, (0, 2, 3, 1)).reshape(bs, D_FOX, past)
    o_s = _attn_sample_call(qb_s, per_batch_t(ktb_s), per_batch_t(vtb_s), feature_major(cache_fox_k[0]),
                            feature_major(cache_fox_v[0]), fq_col, f_row_s, g_fox_out[0], bs, nq, past)
    u_s_tb = jnp.transpose(u_s.reshape(bs, nq, D_S5), (1, 0, 2)).reshape(n_s, D_S5)
    h0_s = jnp.stack([state_s5_re[0].reshape(bs, N_STATE), state_s5_im[0].reshape(bs, N_STATE)])
    ys_s_tb, st_s = _s5_call(u_s_tb, h0_s, bbd, cbd, acon, s5_d[0], wglu, b_glu[0], g_s5_out[0],
                             bs, nq, "s5_sample")
    ys_s = jnp.transpose(ys_s_tb.reshape(nq, bs, D_S5), (1, 0, 2)).reshape(n_s, D_S5)
    y_s = _ffn_call(xs1, mod_s, 6, True, n_s, 1, g_ffn2[0], wup2, wdn2,
                       premix=(ys_s, pl.BlockSpec((n_s, D_S5), lambda i: (i, 0)), o_s.reshape(n_s, D_FOX), wo, 5),
                       final_g=g_final, name="ffn2_sample")

    heads = lambda a, b, t: a.reshape(1, b, t, FOX_HEADS, FOX_HEAD_DIM)
    heads_t = lambda a, b, t: jnp.transpose(a.reshape(1, b, FOX_HEADS, FOX_HEAD_DIM, t), (0, 1, 4, 2, 3))
    gates_t = lambda a: jnp.transpose(a, (0, 2, 1))[None]
    state = lambda a, b: a[:b].reshape(1, b, S5_GROUPS, S5_STATE)
    return (y_p.reshape(bp, seq, D_MODEL), y_s.reshape(bs, nq, D_MODEL),
            heads_t(kt_p, bp, seq), heads_t(vt_p, bp, seq), gates_t(lft_p),
            state(st_p[0], bp), state(st_p[1], bp),
            heads(k_s, bs, nq), heads(v_s, bs, nq), gates_t(lft_s),
            state(st_s[0], bs), state(st_s[1], bs))
```

```python
import functools
import math

import jax
import jax.numpy as jnp
from jax import lax
from jax.experimental import pallas as pl
from jax.experimental.pallas import tpu as pltpu

F32 = jnp.float32
BF16 = jnp.bfloat16

D_MODEL = 1024
D_FF = 2816
D_S5 = 512
D_FOX = 512
FOX_HEADS = 8
FOX_HEAD_DIM = 64
S5_GROUPS = 32
S5_GROUP = 16
S5_STATE = 64
N_STATE = S5_GROUPS * S5_STATE
N_MOD = 9
ATTN_SCALE = FOX_HEAD_DIM ** -0.5
NEG_INF = -1e30
EPS = 1e-6
LANES = 128
SUBLANES = 8
VMEM_LIMIT = 56 * 1024 * 1024


def _params(n_axes=1, vmem=VMEM_LIMIT):
    return pltpu.CompilerParams(dimension_semantics=("arbitrary",) * n_axes, vmem_limit_bytes=vmem)


def _resident(shape):
    nd = len(shape)
    return pl.BlockSpec(shape, lambda *_: (0,) * nd, pipeline_mode=pl.Buffered(1))


def _sigmoid(x):
    return 1.0 / (1.0 + jnp.exp(-x))


def _rms(x, g):
    y = x * lax.rsqrt(jnp.mean(x * x, axis=-1, keepdims=True) + EPS)
    return y * g


def _dot(a, b):
    return jnp.dot(a, b, preferred_element_type=F32)


def _dot_nt(a, b):
    return lax.dot_general(a, b, (((1,), (1,)), ((), ())), preferred_element_type=F32)


def _split3(x):
    hi = x.astype(BF16)
    r = x - hi.astype(F32)
    mid = r.astype(BF16)
    lo = (r - mid.astype(F32)).astype(BF16)
    return hi, mid, lo


def _ada_kernel(c_ref, w_ref, b_ref, o_ref):
    c = c_ref[...]
    s = (c * _sigmoid(c)).astype(BF16)
    o_ref[...] = _dot(s, w_ref[...].astype(BF16)) + b_ref[...]


def _ada_call(c_all, w_ada, b_ada):
    rows = c_all.shape[0]
    n = w_ada.shape[1]
    bn = D_MODEL
    return pl.pallas_call(
        _ada_kernel,
        grid=(n // bn,),
        in_specs=[pl.BlockSpec((rows, D_MODEL), lambda j: (0, 0)),
                  pl.BlockSpec((D_MODEL, bn), lambda j: (0, j)),
                  pl.BlockSpec((1, bn), lambda j: (0, j))],
        out_specs=pl.BlockSpec((rows, bn), lambda j: (0, j)),
        out_shape=jax.ShapeDtypeStruct((rows, n), F32),
        compiler_params=_params(),
        name="ada",
    )(c_all, w_ada, b_ada.reshape(1, n))


def _mod_spec(per_row, tm, tpb, k):
    if per_row:
        return pl.BlockSpec((tm, D_MODEL), lambda i: (i, k))
    return pl.BlockSpec((None, 1, D_MODEL), lambda i: (i // tpb, 0, k))


def _ffn_kernel(*refs, premix, final):
    it = iter(refs)
    x_ref = next(it)
    if premix:
        ys_ref, of_ref, wo_ref, g2_ref = next(it), next(it), next(it), next(it)
    sh_ref, sc_ref, gt_ref, gn_ref, wup_ref, wdn_ref = (next(it) for _ in range(6))
    if final:
        gf_ref = next(it)
    o_ref = next(it)

    x = x_ref[...]
    if premix:
        mixed = _dot(ys_ref[...], wo_ref[:D_S5, :]) + _dot(of_ref[...], wo_ref[D_S5:, :])
        x = x + (1.0 + g2_ref[...]) * mixed
    h = _rms(x, gn_ref[...]) * (1.0 + sc_ref[...]) + sh_ref[...]
    gu = _dot(h.astype(BF16), wup_ref[...])
    g = gu[:, :D_FF]
    u = gu[:, D_FF:]
    a = (g * _sigmoid(g) * u).astype(BF16)
    y = _dot(a, wdn_ref[...])
    xn = x + (0.5 * (1.0 + gt_ref[...])) * y
    o_ref[...] = _rms(xn, gf_ref[...]) if final else xn


def _ffn_call(x, mod, k0, per_row, tm, tpb, gn, wup, wdn, *, premix=None, final_g=None, name):
    n = x.shape[0]
    tok = pl.BlockSpec((tm, D_MODEL), lambda i: (i, 0))
    args, specs = [x], [tok]
    if premix is not None:
        ys, ys_spec, of, wo, k2 = premix
        args += [ys, of, wo, mod]
        specs += [ys_spec, pl.BlockSpec((tm, D_FOX), lambda i: (i, 0)), _resident(wo.shape),
                  _mod_spec(per_row, tm, tpb, k2)]
    args += [mod, mod, mod, gn.reshape(1, D_MODEL), wup, wdn]
    specs += [_mod_spec(per_row, tm, tpb, k0), _mod_spec(per_row, tm, tpb, k0 + 1),
              _mod_spec(per_row, tm, tpb, k0 + 2), _resident((1, D_MODEL)),
              _resident(wup.shape), _resident(wdn.shape)]
    final = final_g is not None
    if final:
        args.append(final_g.reshape(1, D_MODEL))
        specs.append(_resident((1, D_MODEL)))
    return pl.pallas_call(
        functools.partial(_ffn_kernel, premix=premix is not None, final=final),
        grid=(n // tm,),
        in_specs=specs,
        out_specs=tok,
        out_shape=jax.ShapeDtypeStruct((n, D_MODEL), F32),
        compiler_params=_params(),
        name=name,
    )(*args)


F_ROWS = 16


def _inproj_kernel(x_ref, sh_ref, sc_ref, gn_ref, wn_ref, wt_ref, bf_ref, *out_refs, prompt):
    h = (_rms(x_ref[...], gn_ref[...]) * (1.0 + sc_ref[...]) + sh_ref[...]).astype(BF16)
    nat = _dot(h, wn_ref[...])
    pt = _dot_nt(wt_ref[...], h)
    q_t, k_t, v_t = pt[:D_FOX], pt[D_FOX:2 * D_FOX], pt[2 * D_FOX:3 * D_FOX]
    z = pt[3 * D_FOX:] + bf_ref[...]
    lf = (jnp.minimum(z, 0.0) - jnp.log1p(jnp.exp(-jnp.abs(z))))[:FOX_HEADS]
    if prompt:
        u_ref, qtb_ref, kb_ref, vtb_ref, kt_ref, vt_ref, lf_ref = out_refs
        u_ref[...] = nat[:, :D_S5]
        qtb_ref[...] = (q_t * (ATTN_SCALE * LOG2E)).astype(BF16)
        kb_ref[...] = nat[:, D_S5:].astype(BF16)
        vtb_ref[...] = v_t.astype(BF16)
        kt_ref[...] = k_t
        vt_ref[...] = v_t
    else:
        u_ref, qb_ref, ktb_ref, vtb_ref, k_ref, v_ref, lf_ref = out_refs
        u_ref[...] = nat[:, :D_S5]
        qb_ref[...] = (nat[:, D_S5:D_S5 + D_FOX] * ATTN_SCALE).astype(BF16)
        ktb_ref[...] = k_t.astype(BF16)
        vtb_ref[...] = v_t.astype(BF16)
        k_ref[...] = nat[:, D_S5 + D_FOX:D_S5 + 2 * D_FOX]
        v_ref[...] = nat[:, D_S5 + 2 * D_FOX:]
    lf_ref[...] = lf


def _inproj_call(x, mod, per_row, tm, tpb, gn, w_nat, w_t, b_f, prompt, name):
    n = x.shape[0]
    nb = n // (tm * tpb)
    seq = tm * tpb
    sd = jax.ShapeDtypeStruct
    tok = lambda w: pl.BlockSpec((tm, w), lambda i: (i, 0))
    if prompt:
        feat = lambda r: pl.BlockSpec((None, r, tm), lambda i: (i // tpb, 0, i % tpb))
        fshape = lambda r, dt: sd((nb, r, seq), dt)
        out_specs = [tok(D_S5), feat(D_FOX), tok(D_FOX), feat(D_FOX), feat(D_FOX), feat(D_FOX), feat(FOX_HEADS)]
        out_shape = [sd((n, D_S5), F32), fshape(D_FOX, BF16), sd((n, D_FOX), BF16),
                     fshape(D_FOX, BF16), fshape(D_FOX, F32), fshape(D_FOX, F32), fshape(FOX_HEADS, F32)]
    else:
        feat = lambda r: pl.BlockSpec((r, tm), lambda i: (0, i))
        out_specs = [tok(D_S5), tok(D_FOX), feat(D_FOX), feat(D_FOX), tok(D_FOX), tok(D_FOX), feat(FOX_HEADS)]
        out_shape = [sd((n, D_S5), F32), sd((n, D_FOX), BF16), sd((D_FOX, n), BF16), sd((D_FOX, n), BF16),
                     sd((n, D_FOX), F32), sd((n, D_FOX), F32), sd((FOX_HEADS, n), F32)]
    return pl.pallas_call(
        functools.partial(_inproj_kernel, prompt=prompt),
        grid=(n // tm,),
        in_specs=[tok(D_MODEL), _mod_spec(per_row, tm, tpb, 3), _mod_spec(per_row, tm, tpb, 4),
                  _resident((1, D_MODEL)), _resident(w_nat.shape), _resident(w_t.shape),
                  _resident((F_ROWS, 1))],
        out_specs=out_specs,
        out_shape=out_shape,
        compiler_params=_params(),
        name=name,
    )(x, mod, mod, gn.reshape(1, D_MODEL), w_nat, w_t, b_f)


CUM_CHUNK = 512


def _cumsum_kernel(x_ref, o_ref):
    t = x_ref.shape[-1]
    row = lax.broadcasted_iota(jnp.int32, (CUM_CHUNK, CUM_CHUNK), 0)
    col = lax.broadcasted_iota(jnp.int32, (CUM_CHUNK, CUM_CHUNK), 1)
    tri = jnp.where(row <= col, 1.0, 0.0).astype(BF16)
    carry = jnp.zeros((x_ref.shape[0], 1), F32)
    for c in range(t // CUM_CHUNK):
        sl = slice(c * CUM_CHUNK, (c + 1) * CUM_CHUNK)
        hi, mid, lo = _split3(x_ref[:, sl])
        cum = (_dot(hi, tri) + _dot(mid, tri)) + _dot(lo, tri) + carry
        o_ref[:, sl] = cum
        carry = cum[:, CUM_CHUNK - 1:CUM_CHUNK]


def _cumsum_call(x_t, name):
    b, h, t = x_t.shape
    spec = pl.BlockSpec((None, h, t), lambda i: (i, 0, 0))
    return pl.pallas_call(
        _cumsum_kernel, grid=(b,), in_specs=[spec], out_specs=spec,
        out_shape=jax.ShapeDtypeStruct((b, h, t), F32), compiler_params=_params(), name=name,
    )(x_t)


ATTN_TQ = 512
ONES_ROWS = 16
LOG2E = math.log2(math.e)


def _attn_kernel(qt_ref, k_ref, vt_ref, fq_ref, fk_ref, g_ref, o_ref,
                 ka_ref, qa_ref, m_ref, acc_ref, r0_ref, r1_ref, s0_ref, s1_ref, ot_scr, *, tq):
    qi = pl.program_id(1)
    pair = 2 * FOX_HEAD_DIM

    @pl.when(qi == 0)
    def _():
        hi, mid, lo = _split3(fk_ref[...] * (-LOG2E))
        hd = lax.broadcasted_iota(jnp.int32, (FOX_HEADS, D_FOX), 0)
        ln = lax.broadcasted_iota(jnp.int32, (FOX_HEADS, D_FOX), 1)
        base = (hd // 2) * pair + 3 * (hd % 2)
        place = lambda t: jnp.where(ln == base + t, 1.0, 0.0).astype(BF16)
        ka_ref[...] = (_dot(hi, place(0)) + _dot(mid, place(1)) + _dot(lo, place(2))).astype(BF16)

    frow = lax.broadcasted_iota(jnp.int32, (pair, tq), 0)
    head0 = frow < FOX_HEAD_DIM
    key = lax.broadcasted_iota(jnp.int32, (tq, tq), 0)
    qry = lax.broadcasted_iota(jnp.int32, (tq, tq), 1)
    causal = key <= qry
    ones_rows = jnp.ones((ONES_ROWS, tq), BF16)

    for hp in range(FOX_HEADS // 2):
        fs = slice(hp * pair, (hp + 1) * pair)
        q2 = qt_ref[fs, :]
        zero = jnp.zeros_like(q2)
        for e in range(2):
            qa_ref[e, :pair, :] = jnp.where(head0 == (e == 0), q2, zero)
            qa_ref[e, pair:, :] = jnp.where((frow >= 3 * e) & (frow < 3 * e + 3), 1.0, 0.0).astype(BF16)
        fqs = tuple(fq_ref[2 * hp + e:2 * hp + e + 1, :] * LOG2E for e in range(2))
        m_ref[...] = jnp.full(m_ref.shape, NEG_INF, F32)
        acc_ref[...] = jnp.zeros(acc_ref.shape, F32)

        def scores(j, masked, r_ref, st_ref):
            off = pl.multiple_of(j * tq, tq)
            k2 = jnp.concatenate([k_ref[pl.ds(off, tq), fs], ka_ref[pl.ds(off, tq), fs]], axis=1)
            for e in range(2):
                r = _dot(k2, qa_ref[e])
                if masked:
                    r = jnp.where(causal, r, NEG_INF)
                m_prev = m_ref[e]
                m_new = jnp.maximum(m_prev, jnp.max(r, axis=0, keepdims=True) + fqs[e])
                st_ref[e, 0] = jnp.exp2(m_prev - m_new)
                st_ref[e, 1] = fqs[e] - m_new
                m_ref[e] = m_new
                r_ref[e] = r

        def accumulate(j, r_ref, st_ref):
            off = pl.multiple_of(j * tq, tq)
            for e in range(2):
                vt = vt_ref[hp * pair + e * FOX_HEAD_DIM:hp * pair + (e + 1) * FOX_HEAD_DIM, pl.ds(off, tq)]
                p = jnp.exp2(r_ref[e] + st_ref[e, 1]).astype(BF16)
                acc_ref[e] = st_ref[e, 0] * acc_ref[e] + _dot(jnp.concatenate([vt, ones_rows], axis=0), p)

        scores(qi, True, r0_ref, s0_ref)

        def body(i, c):
            j = qi - 2 * i
            scores(j - 1, False, r1_ref, s1_ref)
            accumulate(j, r0_ref, s0_ref)
            scores(j - 2, False, r0_ref, s0_ref)
            accumulate(j - 1, r1_ref, s1_ref)
            return c

        lax.fori_loop(0, qi // 2, body, 0)

        @pl.when(qi % 2 == 1)
        def _():
            scores(0, False, r1_ref, s1_ref)
            accumulate(1, r0_ref, s0_ref)
            accumulate(0, r1_ref, s1_ref)

        @pl.when(qi % 2 == 0)
        def _():
            accumulate(0, r0_ref, s0_ref)

        for e in range(2):
            denom = acc_ref[e, FOX_HEAD_DIM:FOX_HEAD_DIM + 1, :]
            ot_scr[hp * pair + e * FOX_HEAD_DIM:hp * pair + (e + 1) * FOX_HEAD_DIM, :] = (
                acc_ref[e, :FOX_HEAD_DIM, :] / denom)

    ot = ot_scr[...]
    y = ot * lax.rsqrt(jnp.mean(ot * ot, axis=0, keepdims=True) + EPS) * g_ref[...]
    o_ref[...] = y.T.astype(BF16)


def _attn_call(qtb, kb, vtb, f_col, f_row, g_fox, batch, seq):
    tq = ATTN_TQ
    nq = seq // tq
    whole = lambda r, c: pl.BlockSpec((None, r, c), lambda b, i: (b, 0, 0), pipeline_mode=pl.Buffered(1))
    return pl.pallas_call(
        functools.partial(_attn_kernel, tq=tq),
        grid=(batch, nq),
        in_specs=[pl.BlockSpec((None, D_FOX, tq), lambda b, i: (b, 0, i)),
                  whole(seq, D_FOX), whole(D_FOX, seq),
                  pl.BlockSpec((None, FOX_HEADS, tq), lambda b, i: (b, 0, i)),
                  whole(seq, FOX_HEADS),
                  pl.BlockSpec((D_FOX, 1), lambda b, i: (0, 0))],
        out_specs=pl.BlockSpec((None, tq, D_FOX), lambda b, i: (b, i, 0)),
        out_shape=jax.ShapeDtypeStruct((batch, seq, D_FOX), BF16),
        scratch_shapes=[pltpu.VMEM((seq, D_FOX), BF16), pltpu.VMEM((2, 4 * FOX_HEAD_DIM, tq), BF16),
                        pltpu.VMEM((2, 1, tq), F32), pltpu.VMEM((2, FOX_HEAD_DIM + ONES_ROWS, tq), F32),
                        pltpu.VMEM((2, tq, tq), F32), pltpu.VMEM((2, tq, tq), F32),
                        pltpu.VMEM((2, 2, 1, tq), F32), pltpu.VMEM((2, 2, 1, tq), F32),
                        pltpu.VMEM((D_FOX, tq), F32)],
        compiler_params=_params(2),
        name="fox_attn_prompt",
    )(qtb, kb.reshape(batch, seq, D_FOX), vtb, f_row, f_col, g_fox.reshape(D_FOX, 1))


SAMPLE_KC = 1024


def _attn_sample_kernel(q_ref, kn_ref, vn_ref, ck_ref, cv_ref, fq_ref, fk_ref, g_ref, o_ref,
                        qbd_ref, m_ref, l_ref, acc_ref, *, nq, past):
    c = pl.program_id(1)
    nc = pl.num_programs(1)
    rows = FOX_HEADS * nq
    rblk = lax.broadcasted_iota(jnp.int32, (rows, D_FOX), 0) // nq
    cblk = lax.broadcasted_iota(jnp.int32, (rows, D_FOX), 1) // FOX_HEAD_DIM
    own_head = rblk == cblk

    @pl.when(c == 0)
    def _():
        qt = jnp.concatenate([q_ref[...]] * FOX_HEADS, axis=0)
        qbd_ref[...] = jnp.where(own_head, qt, jnp.zeros_like(qt))
        m_ref[...] = jnp.full(m_ref.shape, NEG_INF, F32)
        l_ref[...] = jnp.zeros(l_ref.shape, F32)
        acc_ref[...] = jnp.zeros(acc_ref.shape, F32)

    def expand_heads(f):
        n = f.shape[1]
        return jnp.concatenate([jnp.broadcast_to(f[h:h + 1, :], (nq, n)) for h in range(FOX_HEADS)], axis=0)

    def update(kt, vt, fk, mask):
        s = _dot(qbd_ref[...], kt) + fq_ref[...] - expand_heads(fk)
        if mask is not None:
            s = jnp.where(mask, s, NEG_INF)
        m_prev = m_ref[...]
        m_new = jnp.maximum(m_prev, jnp.max(s, axis=1, keepdims=True))
        alpha = jnp.exp(m_prev - m_new)
        p = jnp.exp(s - m_new)
        l_ref[...] = alpha * l_ref[...] + jnp.sum(p, axis=1, keepdims=True)
        acc_ref[...] = alpha * acc_ref[...] + _dot_nt(p.astype(BF16), vt)
        m_ref[...] = m_new

    off = pl.multiple_of(c * SAMPLE_KC, SAMPLE_KC)
    update(ck_ref[...].astype(BF16), cv_ref[...].astype(BF16), fk_ref[:, pl.ds(off, SAMPLE_KC)], None)

    @pl.when(c == nc - 1)
    def _():
        qpos = lax.broadcasted_iota(jnp.int32, (rows, nq), 0) % nq
        kpos = lax.broadcasted_iota(jnp.int32, (rows, nq), 1)
        update(kn_ref[...], vn_ref[...], fk_ref[:, past:past + nq], kpos <= qpos)
        o_full = jnp.where(own_head, acc_ref[...] / l_ref[...], 0.0)
        o = o_full[0:nq, :]
        for h in range(1, FOX_HEADS):
            o = o + o_full[h * nq:(h + 1) * nq, :]
        o_ref[...] = _rms(o, g_ref[...]).astype(BF16)


def _attn_sample_call(qb, ktb, vtb, cache_kt, cache_vt, fq_col, f_row, g_fox, batch, nq, past):
    rows = FOX_HEADS * nq
    nc = past // SAMPLE_KC
    per_b = lambda w: pl.BlockSpec((None, nq, w), lambda b, c: (b, 0, 0))
    new_t = pl.BlockSpec((None, D_FOX, nq), lambda b, c: (b, 0, 0))
    cache = pl.BlockSpec((None, D_FOX, SAMPLE_KC), lambda b, c: (b, 0, c))
    return pl.pallas_call(
        functools.partial(_attn_sample_kernel, nq=nq, past=past),
        grid=(batch, nc),
        in_specs=[per_b(D_FOX), new_t, new_t, cache, cache,
                  pl.BlockSpec((None, rows, 1), lambda b, c: (b, 0, 0)),
                  pl.BlockSpec((None, FOX_HEADS, f_row.shape[-1]), lambda b, c: (b, 0, 0)),
                  pl.BlockSpec((1, D_FOX), lambda b, c: (0, 0))],
        out_specs=per_b(D_FOX),
        out_shape=jax.ShapeDtypeStruct((batch, nq, D_FOX), BF16),
        scratch_shapes=[pltpu.VMEM((rows, D_FOX), BF16), pltpu.VMEM((rows, 1), F32),
                        pltpu.VMEM((rows, 1), F32), pltpu.VMEM((rows, D_FOX), F32)],
        compiler_params=_params(2),
        name="fox_attn_sample",
    )(qb.reshape(batch, nq, D_FOX), ktb, vtb, cache_kt, cache_vt, fq_col, f_row, g_fox.reshape(1, D_FOX))


def _s5_prep_kernel(lre_ref, lim_ref, ldt_ref, bre_ref, bim_ref, cre_ref, cim_ref,
                    lre_row_ref, lim_row_ref, ldt_row_ref, bbd_ref, cbd_ref, acon_ref):
    def a_bar(lre, lim, ldt):
        dt = jnp.exp(ldt)
        mag = jnp.exp(lre * dt)
        return mag * jnp.cos(lim * dt), mag * jnp.sin(lim * dt)

    lre, lim = lre_ref[...], lim_ref[...]
    ar, ai = a_bar(lre, lim, ldt_ref[...])
    den = lre * lre + lim * lim
    cr = ((ar - 1.0) * lre + ai * lim) / den
    ci = (ai * lre - (ar - 1.0) * lim) / den
    bre, bim = bre_ref[...], bim_ref[...]
    b_r = (cr * bre - ci * bim).astype(BF16)
    b_i = (cr * bim + ci * bre).astype(BF16)
    srow = lax.broadcasted_iota(jnp.int32, (S5_STATE, N_STATE), 0)
    scol = lax.broadcasted_iota(jnp.int32, (S5_STATE, N_STATE), 1)
    rep_state = jnp.where(scol % S5_STATE == srow, 1.0, 0.0).astype(BF16)
    grow = lax.broadcasted_iota(jnp.int32, (D_S5, N_STATE), 0) // S5_GROUP
    gcol = lax.broadcasted_iota(jnp.int32, (D_S5, N_STATE), 1) // S5_STATE
    own = grow == gcol
    bbd_ref[:, :N_STATE] = jnp.where(own, _dot(b_r, rep_state), 0.0).astype(BF16)
    bbd_ref[:, N_STATE:] = jnp.where(own, _dot(b_i, rep_state), 0.0).astype(BF16)

    crow = lax.broadcasted_iota(jnp.int32, (S5_GROUP, D_S5), 0)
    ccol = lax.broadcasted_iota(jnp.int32, (S5_GROUP, D_S5), 1)
    rep_chan = jnp.where(ccol % S5_GROUP == crow, 1.0, 0.0).astype(BF16)
    grow2 = lax.broadcasted_iota(jnp.int32, (N_STATE, D_S5), 0) // S5_STATE
    gcol2 = lax.broadcasted_iota(jnp.int32, (N_STATE, D_S5), 1) // S5_GROUP
    own2 = grow2 == gcol2
    cbd_ref[:N_STATE, :] = jnp.where(own2, _dot(cre_ref[...].astype(BF16), rep_chan), 0.0).astype(BF16)
    cbd_ref[N_STATE:, :] = jnp.where(own2, -_dot(cim_ref[...].astype(BF16), rep_chan), 0.0).astype(BF16)

    ar1, ai1 = a_bar(lre_row_ref[...], lim_row_ref[...], ldt_row_ref[...])
    ar2 = ar1 * ar1 - ai1 * ai1
    ai2 = 2.0 * ar1 * ai1
    upper = lax.broadcasted_iota(jnp.int32, (SUBLANES, N_STATE), 0) < SUBLANES // 2
    bc = lambda v: jnp.broadcast_to(v, (SUBLANES, N_STATE))
    acon_ref[0] = bc(ar1)
    acon_ref[1] = bc(ai1)
    acon_ref[2] = jnp.where(upper, bc(ar1), bc(ar2))
    acon_ref[3] = jnp.where(upper, bc(ai1), bc(ai2))
    acon_ref[4] = jnp.where(upper, 0.0, bc(ar1))
    acon_ref[5] = jnp.where(upper, 0.0, bc(ai1))


def _s5_prep_call(lam_re, lam_im, log_dt, b_re, b_im, c_re, c_im):
    rep_rows = lambda a: jnp.repeat(a, S5_GROUP, axis=0)
    ldt_gp = jnp.broadcast_to(log_dt[:, None], (S5_GROUPS, S5_STATE))
    to_cp = lambda b: jnp.transpose(b, (0, 2, 1)).reshape(D_S5, S5_STATE)
    to_pc = lambda c: jnp.transpose(c, (0, 2, 1)).reshape(N_STATE, S5_GROUP)
    row = lambda a: a.reshape(1, N_STATE)
    args = (rep_rows(lam_re), rep_rows(lam_im), rep_rows(ldt_gp), to_cp(b_re), to_cp(b_im),
            to_pc(c_re), to_pc(c_im), row(lam_re), row(lam_im), row(ldt_gp))
    sd = jax.ShapeDtypeStruct
    return pl.pallas_call(
        _s5_prep_kernel,
        out_shape=[sd((D_S5, 2 * N_STATE), BF16), sd((2 * N_STATE, D_S5), BF16),
                   sd((6, SUBLANES, N_STATE), F32)],
        compiler_params=pltpu.CompilerParams(vmem_limit_bytes=VMEM_LIMIT),
        name="s5_prep",
    )(*args)


SCAN_COLS = 512


def _s5_kernel(u_ref, h0_ref, bbd_ref, cbd_ref, acon_ref, d_ref, wglu_ref, bglu_ref, gs_ref,
               y_ref, st_ref, ui_ref, s_ref, yi_ref, h_ref, *, nb):
    i = pl.program_id(0)
    tt = u_ref.shape[1]
    rows = nb * tt
    per = SUBLANES // nb
    half_in = D_S5 // 2
    half_st = N_STATE // 2

    @pl.when(i == 0)
    def _():
        h_ref[...] = h0_ref[...]

    strips = D_S5 // LANES
    for b in range(nb):
        for c in range(strips):
            yi_ref[c, b * tt:(b + 1) * tt, :] = u_ref[b, :, c * LANES:(c + 1) * LANES]
    for k in range(rows // SUBLANES):
        for c in range(strips):
            parts = [yi_ref[c, pl.ds(k * per + s, nb, stride=tt), :] for s in range(per)]
            ui_ref[k * SUBLANES:(k + 1) * SUBLANES, c * LANES:(c + 1) * LANES] = (
                parts[0] if per == 1 else jnp.concatenate(parts, axis=0))

    ub = ui_ref[...].astype(BF16)
    for h in range(2):
        x = ub[:, h * half_in:(h + 1) * half_in]
        for part in range(2):
            cols = slice(part * N_STATE + h * half_st, part * N_STATE + (h + 1) * half_st)
            s_ref[:, cols] = _dot(x, bbd_ref[h * half_in:(h + 1) * half_in, cols])

    upper = lax.broadcasted_iota(jnp.int32, (SUBLANES, SCAN_COLS), 0) < SUBLANES // 2
    for c in range(N_STATE // SCAN_COLS):
        re = slice(c * SCAN_COLS, (c + 1) * SCAN_COLS)
        im = slice(N_STATE + c * SCAN_COLS, N_STATE + (c + 1) * SCAN_COLS)
        if nb == SUBLANES:
            ar, ai = acon_ref[0, :, re], acon_ref[1, :, re]

            def step(k, carry):
                hr, hi = carry
                r0 = pl.multiple_of(k * SUBLANES, SUBLANES)
                nr = ar * hr - ai * hi + s_ref[pl.ds(r0, SUBLANES), re]
                ni = ar * hi + ai * hr + s_ref[pl.ds(r0, SUBLANES), im]
                s_ref[pl.ds(r0, SUBLANES), re] = nr
                s_ref[pl.ds(r0, SUBLANES), im] = ni
                return nr, ni
        else:
            a2r, a2i = acon_ref[2, :, re], acon_ref[3, :, re]
            alr, ali = acon_ref[4, :, re], acon_ref[5, :, re]

            def step(k, carry):
                hr, hi = carry
                r0 = pl.multiple_of(k * SUBLANES, SUBLANES)
                xr = s_ref[pl.ds(r0, SUBLANES), re]
                xi = s_ref[pl.ds(r0, SUBLANES), im]
                sr = pltpu.roll(xr, SUBLANES // 2, 0)
                si = pltpu.roll(xi, SUBLANES // 2, 0)
                nr = (a2r * hr - a2i * hi) + (alr * sr - ali * si) + xr
                ni = (a2r * hi + a2i * hr) + (alr * si + ali * sr) + xi
                s_ref[pl.ds(r0, SUBLANES), re] = nr
                s_ref[pl.ds(r0, SUBLANES), im] = ni
                return (jnp.where(upper, pltpu.roll(nr, SUBLANES // 2, 0), nr),
                        jnp.where(upper, pltpu.roll(ni, SUBLANES // 2, 0), ni))

        hr, hi = lax.fori_loop(0, rows // SUBLANES, step, (h_ref[0, :, re], h_ref[1, :, re]), unroll=2)
        h_ref[0, :, re] = hr
        h_ref[1, :, re] = hi

    @pl.when(i == pl.num_programs(0) - 1)
    def _():
        st_ref[...] = h_ref[...]

    halves = []
    for h in range(2):
        oc = slice(h * half_in, (h + 1) * half_in)
        sr = slice(h * half_st, (h + 1) * half_st)
        si = slice(N_STATE + h * half_st, N_STATE + (h + 1) * half_st)
        halves.append(_dot(s_ref[:, sr].astype(BF16), cbd_ref[sr, oc])
                      + _dot(s_ref[:, si].astype(BF16), cbd_ref[si, oc]))
    y = jnp.concatenate(halves, axis=1) + d_ref[...] * ui_ref[...]
    y = 0.5 * y * (1.0 + jnp.tanh(math.sqrt(2.0 / math.pi) * (y + 0.044715 * (y * y * y))))
    y = y * _sigmoid(_dot(y.astype(BF16), wglu_ref[...]) + bglu_ref[...])
    y = _rms(y, gs_ref[...])
    for c in range(strips):
        yi_ref[c] = y[:, c * LANES:(c + 1) * LANES]
    for b in range(nb):
        y_ref[b] = jnp.concatenate([yi_ref[c, pl.ds(b, tt, stride=nb), :] for c in range(strips)],
                                   axis=1).astype(BF16)


def _s5_call(u, h0, bbd, cbd, acon, s5_d, wglu, b_glu, g_s5, tt, name):
    nb, seq, _ = u.shape
    rows = tt * nb
    blk = pl.BlockSpec((nb, tt, D_S5), lambda i: (0, i, 0))
    vec = lambda a: a.reshape(1, D_S5)
    return pl.pallas_call(
        functools.partial(_s5_kernel, nb=nb),
        grid=(seq // tt,),
        in_specs=[blk, _resident(h0.shape), _resident(bbd.shape), _resident(cbd.shape),
                  _resident(acon.shape), _resident((1, D_S5)), _resident(wglu.shape),
                  _resident((1, D_S5)), _resident((1, D_S5))],
        out_specs=[blk, pl.BlockSpec(h0.shape, lambda i: (0, 0, 0))],
        out_shape=[jax.ShapeDtypeStruct((nb, seq, D_S5), BF16), jax.ShapeDtypeStruct(h0.shape, F32)],
        scratch_shapes=[pltpu.VMEM((rows, D_S5), F32), pltpu.VMEM((rows, 2 * N_STATE), F32),
                        pltpu.VMEM((D_S5 // LANES, rows, LANES), F32), pltpu.VMEM(h0.shape, F32)],
        compiler_params=_params(),
        name=name,
    )(u, h0, bbd, cbd, acon, vec(s5_d), wglu, vec(b_glu), vec(g_s5))


FFN_TM = 512
S5_TT = 128


def kernel(x_prompt, x_sample, c_prompt, c_sample, cache_fox_k, cache_fox_v, cache_fox_logf, state_s5_re, state_s5_im, w_ada, b_ada, g_ffn1, w_up1, w_down1, g_mix, w_in, b_fgate, lam_re, lam_im, log_dt, s5_b_re, s5_b_im, s5_c_re, s5_c_im, s5_d, w_glu, b_glu, g_s5_out, g_fox_out, w_out, g_ffn2, w_up2, w_down2, g_final):
    assert w_ada.shape[0] == 1, "single-layer trunk"
    bp, seq, _ = x_prompt.shape
    bs, nq, _ = x_sample.shape
    past = cache_fox_k.shape[2]
    n_p, n_s = bp * seq, bs * nq

    wup1, wdn1 = w_up1[0].astype(BF16), w_down1[0].astype(BF16)
    wup2, wdn2 = w_up2[0].astype(BF16), w_down2[0].astype(BF16)
    w_inb = w_in[0].astype(BF16)
    w_nat_s = w_inb[:, :D_S5 + 3 * D_FOX]
    w_nat_p = jnp.concatenate([w_inb[:, :D_S5], w_inb[:, D_S5 + D_FOX:D_S5 + 2 * D_FOX]], axis=1)
    w_t = jnp.pad(w_inb[:, D_S5:].T, ((0, F_ROWS - FOX_HEADS), (0, 0)))
    b_f = jnp.pad(b_fgate[0], (0, F_ROWS - FOX_HEADS)).reshape(F_ROWS, 1)
    wo = w_out[0].astype(BF16)
    wglu = w_glu[0].astype(BF16)

    pad_rows = -(bp + bs) % SUBLANES
    c_all = jnp.concatenate([c_prompt, c_sample, jnp.zeros((pad_rows, D_MODEL), F32)], axis=0)
    mod = _ada_call(c_all, w_ada[0], b_ada[0])
    mod_p = mod[:bp].reshape(bp, 1, N_MOD * D_MODEL)
    mod_s = jnp.repeat(mod[bp:bp + bs], nq, axis=0)

    bbd, cbd, acon = _s5_prep_call(lam_re[0], lam_im[0], log_dt[0], s5_b_re[0], s5_b_im[0],
                                   s5_c_re[0], s5_c_im[0])

    tm = FFN_TM
    tpb = seq // tm
    xp = x_prompt.reshape(n_p, D_MODEL)
    x1 = _ffn_call(xp, mod_p, 0, False, tm, tpb, g_ffn1[0], wup1, wdn1, name="ffn1_prompt")
    u_p, qtb, kb, vtb, kt_p, vt_p, lft_p = _inproj_call(
        x1, mod_p, False, tm, tpb, g_mix[0], w_nat_p, w_t, b_f, True, "inproj_prompt")
    f_row = _cumsum_call(lft_p, "cumsum_prompt")
    f_col = jnp.transpose(f_row, (0, 2, 1))
    o_p = _attn_call(qtb, kb, vtb, f_col, f_row, g_fox_out[0], bp, seq)
    h0_p = jnp.zeros((2, SUBLANES, N_STATE), F32)
    ys_p, st_p = _s5_call(u_p.reshape(bp, seq, D_S5), h0_p, bbd, cbd, acon, s5_d[0], wglu, b_glu[0],
                          g_s5_out[0], S5_TT, "s5_prompt")
    y_p = _ffn_call(x1, mod_p, 6, False, tm, tpb, g_ffn2[0], wup2, wdn2,
                    premix=(ys_p.reshape(n_p, D_S5), pl.BlockSpec((tm, D_S5), lambda i: (i, 0)),
                            o_p.reshape(n_p, D_FOX), wo, 5),
                    final_g=g_final, name="ffn2_prompt")

    xs = x_sample.reshape(n_s, D_MODEL)
    xs1 = _ffn_call(xs, mod_s, 0, True, n_s, 1, g_ffn1[0], wup1, wdn1, name="ffn1_sample")
    u_s, qb_s, ktb_s, vtb_s, k_s, v_s, lft_s = _inproj_call(
        xs1, mod_s, True, n_s, 1, g_mix[0], w_nat_s, w_t, b_f, False, "inproj_sample")
    per_batch_t = lambda a: jnp.transpose(a.reshape(a.shape[0], bs, nq), (1, 0, 2))
    lft_s = per_batch_t(lft_s)
    f_pad = -(past + nq) % CUM_CHUNK
    lft_all = jnp.concatenate([jnp.transpose(cache_fox_logf[0], (0, 2, 1)), lft_s,
                               jnp.zeros((bs, FOX_HEADS, f_pad), F32)], axis=2)
    f_row_s = _cumsum_call(lft_all, "cumsum_sample")
    fq_col = f_row_s[:, :, past:past + nq].reshape(bs, FOX_HEADS * nq, 1)
    feature_major = lambda c: jnp.transpose(c, (0, 2, 3, 1)).reshape(bs, D_FOX, past)
    o_s = _attn_sample_call(qb_s, per_batch_t(ktb_s), per_batch_t(vtb_s), feature_major(cache_fox_k[0]),
                            feature_major(cache_fox_v[0]), fq_col, f_row_s, g_fox_out[0], bs, nq, past)
    h0_s = jnp.stack([state_s5_re[0].reshape(bs, N_STATE), state_s5_im[0].reshape(bs, N_STATE)])
    ys_s, st_s = _s5_call(u_s.reshape(bs, nq, D_S5), h0_s, bbd, cbd, acon, s5_d[0], wglu, b_glu[0],
                          g_s5_out[0], nq, "s5_sample")
    y_s = _ffn_call(xs1, mod_s, 6, True, n_s, 1, g_ffn2[0], wup2, wdn2,
                    premix=(ys_s.reshape(n_s, D_S5), pl.BlockSpec((n_s, D_S5), lambda i: (i, 0)),
                            o_s.reshape(n_s, D_FOX), wo, 5),
                    final_g=g_final, name="ffn2_sample")

    heads = lambda a, b, t: a.reshape(1, b, t, FOX_HEADS, FOX_HEAD_DIM)
    heads_t = lambda a, b, t: jnp.transpose(a.reshape(1, b, FOX_HEADS, FOX_HEAD_DIM, t), (0, 1, 4, 2, 3))
    gates_t = lambda a: jnp.transpose(a, (0, 2, 1))[None]
    state = lambda a, b: a[:b].reshape(1, b, S5_GROUPS, S5_STATE)
    return (y_p.reshape(bp, seq, D_MODEL), y_s.reshape(bs, nq, D_MODEL),
            heads_t(kt_p, bp, seq), heads_t(vt_p, bp, seq), gates_t(lft_p),
            state(st_p[0], bp), state(st_p[1], bp),
            heads(k_s, bs, nq), heads(v_s, bs, nq), gates_t(lft_s),
            state(st_s[0], bs), state(st_s[1], bs))
```

```python
import functools
import math

import jax
import jax.numpy as jnp
from jax import lax
from jax.experimental import pallas as pl
from jax.experimental.pallas import tpu as pltpu

F32 = jnp.float32
BF16 = jnp.bfloat16

D_MODEL = 1024
D_FF = 2816
D_S5 = 512
D_FOX = 512
FOX_HEADS = 8
FOX_HEAD_DIM = 64
S5_GROUPS = 32
S5_GROUP = 16
S5_STATE = 64
N_STATE = S5_GROUPS * S5_STATE
N_MOD = 9
ATTN_SCALE = FOX_HEAD_DIM ** -0.5
NEG_INF = -1e30
EPS = 1e-6
LANES = 128
SUBLANES = 8
VMEM_LIMIT = 56 * 1024 * 1024


def _params(n_axes=1, vmem=VMEM_LIMIT):
    return pltpu.CompilerParams(dimension_semantics=("arbitrary",) * n_axes, vmem_limit_bytes=vmem)


def _resident(shape):
    nd = len(shape)
    return pl.BlockSpec(shape, lambda *_: (0,) * nd, pipeline_mode=pl.Buffered(1))


def _sigmoid(x):
    return 1.0 / (1.0 + jnp.exp(-x))


def _rms(x, g):
    y = x * lax.rsqrt(jnp.mean(x * x, axis=-1, keepdims=True) + EPS)
    return y * g


def _dot(a, b):
    return jnp.dot(a, b, preferred_element_type=F32)


def _dot_nt(a, b):
    return lax.dot_general(a, b, (((1,), (1,)), ((), ())), preferred_element_type=F32)


def _split3(x):
    hi = x.astype(BF16)
    r = x - hi.astype(F32)
    mid = r.astype(BF16)
    lo = (r - mid.astype(F32)).astype(BF16)
    return hi, mid, lo


def _ada_kernel(c_ref, w_ref, b_ref, o_ref):
    c = c_ref[...]
    s = (c * _sigmoid(c)).astype(BF16)
    o_ref[...] = _dot(s, w_ref[...].astype(BF16)) + b_ref[...]


def _ada_call(c_all, w_ada, b_ada):
    rows = c_all.shape[0]
    n = w_ada.shape[1]
    bn = D_MODEL
    return pl.pallas_call(
        _ada_kernel,
        grid=(n // bn,),
        in_specs=[pl.BlockSpec((rows, D_MODEL), lambda j: (0, 0)),
                  pl.BlockSpec((D_MODEL, bn), lambda j: (0, j)),
                  pl.BlockSpec((1, bn), lambda j: (0, j))],
        out_specs=pl.BlockSpec((rows, bn), lambda j: (0, j)),
        out_shape=jax.ShapeDtypeStruct((rows, n), F32),
        compiler_params=_params(),
        name="ada",
    )(c_all, w_ada, b_ada.reshape(1, n))


def _mod_spec(per_row, tm, tpb, k):
    if per_row:
        return pl.BlockSpec((tm, D_MODEL), lambda i: (i, k))
    return pl.BlockSpec((None, 1, D_MODEL), lambda i: (i // tpb, 0, k))


def _ffn_kernel(*refs, premix, final):
    it = iter(refs)
    x_ref = next(it)
    if premix:
        ys_ref, of_ref, wo_ref, g2_ref = next(it), next(it), next(it), next(it)
    sh_ref, sc_ref, gt_ref, gn_ref, wup_ref, wdn_ref = (next(it) for _ in range(6))
    if final:
        gf_ref = next(it)
    o_ref = next(it)

    x = x_ref[...]
    if premix:
        mixed = _dot(ys_ref[...], wo_ref[:D_S5, :]) + _dot(of_ref[...], wo_ref[D_S5:, :])
        x = x + (1.0 + g2_ref[...]) * mixed
    h = _rms(x, gn_ref[...]) * (1.0 + sc_ref[...]) + sh_ref[...]
    gu = _dot(h.astype(BF16), wup_ref[...])
    g = gu[:, :D_FF]
    u = gu[:, D_FF:]
    a = (g * _sigmoid(g) * u).astype(BF16)
    y = _dot(a, wdn_ref[...])
    xn = x + (0.5 * (1.0 + gt_ref[...])) * y
    o_ref[...] = _rms(xn, gf_ref[...]) if final else xn


def _ffn_call(x, mod, k0, per_row, tm, tpb, gn, wup, wdn, *, premix=None, final_g=None, name):
    n = x.shape[0]
    tok = pl.BlockSpec((tm, D_MODEL), lambda i: (i, 0))
    args, specs = [x], [tok]
    if premix is not None:
        ys, ys_spec, of, wo, k2 = premix
        args += [ys, of, wo, mod]
        specs += [ys_spec, pl.BlockSpec((tm, D_FOX), lambda i: (i, 0)), _resident(wo.shape),
                  _mod_spec(per_row, tm, tpb, k2)]
    args += [mod, mod, mod, gn.reshape(1, D_MODEL), wup, wdn]
    specs += [_mod_spec(per_row, tm, tpb, k0), _mod_spec(per_row, tm, tpb, k0 + 1),
              _mod_spec(per_row, tm, tpb, k0 + 2), _resident((1, D_MODEL)),
              _resident(wup.shape), _resident(wdn.shape)]
    final = final_g is not None
    if final:
        args.append(final_g.reshape(1, D_MODEL))
        specs.append(_resident((1, D_MODEL)))
    return pl.pallas_call(
        functools.partial(_ffn_kernel, premix=premix is not None, final=final),
        grid=(n // tm,),
        in_specs=specs,
        out_specs=tok,
        out_shape=jax.ShapeDtypeStruct((n, D_MODEL), F32),
        compiler_params=_params(),
        name=name,
    )(*args)


F_ROWS = 16


def _inproj_kernel(x_ref, sh_ref, sc_ref, gn_ref, wn_ref, wt_ref, bf_ref, *out_refs, prompt):
    h = (_rms(x_ref[...], gn_ref[...]) * (1.0 + sc_ref[...]) + sh_ref[...]).astype(BF16)
    nat = _dot(h, wn_ref[...])
    pt = _dot_nt(wt_ref[...], h)
    q_t, k_t, v_t = pt[:D_FOX], pt[D_FOX:2 * D_FOX], pt[2 * D_FOX:3 * D_FOX]
    z = pt[3 * D_FOX:] + bf_ref[...]
    lf = (jnp.minimum(z, 0.0) - jnp.log1p(jnp.exp(-jnp.abs(z))))[:FOX_HEADS]
    if prompt:
        u_ref, qtb_ref, kb_ref, vtb_ref, kt_ref, vt_ref, lf_ref = out_refs
        u_ref[...] = nat[:, :D_S5]
        qtb_ref[...] = (q_t * (ATTN_SCALE * LOG2E)).astype(BF16)
        kb_ref[...] = nat[:, D_S5:].astype(BF16)
        vtb_ref[...] = v_t.astype(BF16)
        kt_ref[...] = k_t
        vt_ref[...] = v_t
    else:
        u_ref, qb_ref, ktb_ref, vtb_ref, k_ref, v_ref, lf_ref = out_refs
        u_ref[...] = nat[:, :D_S5]
        qb_ref[...] = (nat[:, D_S5:D_S5 + D_FOX] * ATTN_SCALE).astype(BF16)
        ktb_ref[...] = k_t.astype(BF16)
        vtb_ref[...] = v_t.astype(BF16)
        k_ref[...] = nat[:, D_S5 + D_FOX:D_S5 + 2 * D_FOX]
        v_ref[...] = nat[:, D_S5 + 2 * D_FOX:]
    lf_ref[...] = lf


def _inproj_call(x, mod, per_row, tm, tpb, gn, w_nat, w_t, b_f, prompt, name):
    n = x.shape[0]
    nb = n // (tm * tpb)
    seq = tm * tpb
    sd = jax.ShapeDtypeStruct
    tok = lambda w: pl.BlockSpec((tm, w), lambda i: (i, 0))
    if prompt:
        feat = lambda r: pl.BlockSpec((None, r, tm), lambda i: (i // tpb, 0, i % tpb))
        fshape = lambda r, dt: sd((nb, r, seq), dt)
        out_specs = [tok(D_S5), feat(D_FOX), tok(D_FOX), feat(D_FOX), feat(D_FOX), feat(D_FOX), feat(FOX_HEADS)]
        out_shape = [sd((n, D_S5), F32), fshape(D_FOX, BF16), sd((n, D_FOX), BF16),
                     fshape(D_FOX, BF16), fshape(D_FOX, F32), fshape(D_FOX, F32), fshape(FOX_HEADS, F32)]
    else:
        feat = lambda r: pl.BlockSpec((r, tm), lambda i: (0, i))
        out_specs = [tok(D_S5), tok(D_FOX), feat(D_FOX), feat(D_FOX), tok(D_FOX), tok(D_FOX), feat(FOX_HEADS)]
        out_shape = [sd((n, D_S5), F32), sd((n, D_FOX), BF16), sd((D_FOX, n), BF16), sd((D_FOX, n), BF16),
                     sd((n, D_FOX), F32), sd((n, D_FOX), F32), sd((FOX_HEADS, n), F32)]
    return pl.pallas_call(
        functools.partial(_inproj_kernel, prompt=prompt),
        grid=(n // tm,),
        in_specs=[tok(D_MODEL), _mod_spec(per_row, tm, tpb, 3), _mod_spec(per_row, tm, tpb, 4),
                  _resident((1, D_MODEL)), _resident(w_nat.shape), _resident(w_t.shape),
                  _resident((F_ROWS, 1))],
        out_specs=out_specs,
        out_shape=out_shape,
        compiler_params=_params(),
        name=name,
    )(x, mod, mod, gn.reshape(1, D_MODEL), w_nat, w_t, b_f)


CUM_CHUNK = 512


def _cumsum_kernel(x_ref, o_ref):
    t = x_ref.shape[-1]
    row = lax.broadcasted_iota(jnp.int32, (CUM_CHUNK, CUM_CHUNK), 0)
    col = lax.broadcasted_iota(jnp.int32, (CUM_CHUNK, CUM_CHUNK), 1)
    tri = jnp.where(row <= col, 1.0, 0.0).astype(BF16)
    carry = jnp.zeros((x_ref.shape[0], 1), F32)
    for c in range(t // CUM_CHUNK):
        sl = slice(c * CUM_CHUNK, (c + 1) * CUM_CHUNK)
        hi, mid, lo = _split3(x_ref[:, sl])
        cum = (_dot(hi, tri) + _dot(mid, tri)) + _dot(lo, tri) + carry
        o_ref[:, sl] = cum
        carry = cum[:, CUM_CHUNK - 1:CUM_CHUNK]


def _cumsum_call(x_t, name):
    b, h, t = x_t.shape
    spec = pl.BlockSpec((None, h, t), lambda i: (i, 0, 0))
    return pl.pallas_call(
        _cumsum_kernel, grid=(b,), in_specs=[spec], out_specs=spec,
        out_shape=jax.ShapeDtypeStruct((b, h, t), F32), compiler_params=_params(), name=name,
    )(x_t)


ATTN_TQ = 512
ONES_ROWS = 16
LOG2E = math.log2(math.e)


def _attn_kernel(qt_ref, k_ref, vt_ref, fq_ref, fk_ref, g_ref, o_ref,
                 ka_ref, qa_ref, fq2_ref, m_ref, acc_ref, ot_scr, *rs_refs, tq):
    r_refs, s_refs = rs_refs[:FOX_HEADS // 2], rs_refs[FOX_HEADS // 2:]
    qi = pl.program_id(1)
    pair = 2 * FOX_HEAD_DIM

    @pl.when(qi == 0)
    def _():
        hi, mid, lo = _split3(fk_ref[...] * (-LOG2E))
        hd = lax.broadcasted_iota(jnp.int32, (FOX_HEADS, D_FOX), 0)
        ln = lax.broadcasted_iota(jnp.int32, (FOX_HEADS, D_FOX), 1)
        base = (hd // 2) * pair + 3 * (hd % 2)
        place = lambda t: jnp.where(ln == base + t, 1.0, 0.0).astype(BF16)
        ka_ref[...] = (_dot(hi, place(0)) + _dot(mid, place(1)) + _dot(lo, place(2))).astype(BF16)

    frow = lax.broadcasted_iota(jnp.int32, (pair, tq), 0)
    head0 = frow < FOX_HEAD_DIM
    key = lax.broadcasted_iota(jnp.int32, (tq, tq), 0)
    qry = lax.broadcasted_iota(jnp.int32, (tq, tq), 1)
    causal = key <= qry
    ones_rows = jnp.ones((ONES_ROWS, tq), BF16)

    n_pairs = FOX_HEADS // 2

    for hp in range(n_pairs):
        q2 = qt_ref[hp * pair:(hp + 1) * pair, :]
        zero = jnp.zeros_like(q2)
        for e in range(2):
            qa_ref[hp, e, :pair, :] = jnp.where(head0 == (e == 0), q2, zero)
            qa_ref[hp, e, pair:, :] = jnp.where((frow >= 3 * e) & (frow < 3 * e + 3), 1.0, 0.0).astype(BF16)
    fq2_ref[...] = fq_ref[...] * LOG2E
    m_ref[...] = jnp.full(m_ref.shape, NEG_INF, F32)
    acc_ref[...] = jnp.zeros(acc_ref.shape, F32)

    def scores(j, hp, masked, r_ref, st_ref):
        off = pl.multiple_of(j * tq, tq)
        fs = slice(hp * pair, (hp + 1) * pair)
        k2 = jnp.concatenate([k_ref[pl.ds(off, tq), fs], ka_ref[pl.ds(off, tq), fs]], axis=1)
        for e in range(2):
            fq = fq2_ref[2 * hp + e:2 * hp + e + 1, :]
            r = _dot(k2, qa_ref[hp, e])
            if masked:
                r = jnp.where(causal, r, NEG_INF)
            m_prev = m_ref[hp, e]
            m_new = jnp.maximum(m_prev, jnp.max(r, axis=0, keepdims=True) + fq)
            st_ref[e, 0] = jnp.exp2(m_prev - m_new)
            st_ref[e, 1] = fq - m_new
            m_ref[hp, e] = m_new
            r_ref[e] = r

    def accumulate(j, hp, r_ref, st_ref):
        off = pl.multiple_of(j * tq, tq)
        for e in range(2):
            vt = vt_ref[hp * pair + e * FOX_HEAD_DIM:hp * pair + (e + 1) * FOX_HEAD_DIM, pl.ds(off, tq)]
            p = jnp.exp2(r_ref[e] + st_ref[e, 1]).astype(BF16)
            acc_ref[hp, e] = st_ref[e, 0] * acc_ref[hp, e] + _dot(jnp.concatenate([vt, ones_rows], axis=0), p)

    bufs = tuple(zip(r_refs, s_refs))

    def sweep(j, masked, pending_j):
        for hp in range(n_pairs):
            scores(j, hp, masked, *bufs[hp])
            if hp > 0:
                accumulate(j, hp - 1, *bufs[hp - 1])
            elif pending_j is not None:
                accumulate(pending_j, n_pairs - 1, *bufs[n_pairs - 1])

    sweep(qi, True, None)

    def body(s, c):
        sweep(qi - s, False, qi - s + 1)
        return c

    lax.fori_loop(1, qi + 1, body, 0)
    accumulate(0, n_pairs - 1, *bufs[n_pairs - 1])

    for hp in range(n_pairs):
        for e in range(2):
            denom = acc_ref[hp, e, FOX_HEAD_DIM:FOX_HEAD_DIM + 1, :]
            ot_scr[hp * pair + e * FOX_HEAD_DIM:hp * pair + (e + 1) * FOX_HEAD_DIM, :] = (
                acc_ref[hp, e, :FOX_HEAD_DIM, :] / denom)

    ot = ot_scr[...]
    y = ot * lax.rsqrt(jnp.mean(ot * ot, axis=0, keepdims=True) + EPS) * g_ref[...]
    o_ref[...] = y.T.astype(BF16)


def _attn_call(qtb, kb, vtb, f_col, f_row, g_fox, batch, seq):
    tq = ATTN_TQ
    nq = seq // tq
    n_pairs = FOX_HEADS // 2
    whole = lambda r, c: pl.BlockSpec((None, r, c), lambda b, i: (b, 0, 0), pipeline_mode=pl.Buffered(1))
    return pl.pallas_call(
        functools.partial(_attn_kernel, tq=tq),
        grid=(batch, nq),
        in_specs=[pl.BlockSpec((None, D_FOX, tq), lambda b, i: (b, 0, i)),
                  whole(seq, D_FOX), whole(D_FOX, seq),
                  pl.BlockSpec((None, FOX_HEADS, tq), lambda b, i: (b, 0, i)),
                  whole(seq, FOX_HEADS),
                  pl.BlockSpec((D_FOX, 1), lambda b, i: (0, 0))],
        out_specs=pl.BlockSpec((None, tq, D_FOX), lambda b, i: (b, i, 0)),
        out_shape=jax.ShapeDtypeStruct((batch, seq, D_FOX), BF16),
        scratch_shapes=[pltpu.VMEM((seq, D_FOX), BF16), pltpu.VMEM((n_pairs, 2, 4 * FOX_HEAD_DIM, tq), BF16),
                        pltpu.VMEM((FOX_HEADS, tq), F32), pltpu.VMEM((n_pairs, 2, 1, tq), F32),
                        pltpu.VMEM((n_pairs, 2, FOX_HEAD_DIM + ONES_ROWS, tq), F32),
                        pltpu.VMEM((D_FOX, tq), F32)]
        + [pltpu.VMEM((2, tq, tq), F32)] * n_pairs + [pltpu.VMEM((2, 2, 1, tq), F32)] * n_pairs,
        compiler_params=_params(2),
        name="fox_attn_prompt",
    )(qtb, kb.reshape(batch, seq, D_FOX), vtb, f_row, f_col, g_fox.reshape(D_FOX, 1))


SAMPLE_KC = 1024


def _attn_sample_kernel(q_ref, kn_ref, vn_ref, ck_ref, cv_ref, fq_ref, fk_ref, g_ref, o_ref,
                        qbd_ref, m_ref, l_ref, acc_ref, *, nq, past):
    c = pl.program_id(1)
    nc = pl.num_programs(1)
    rows = FOX_HEADS * nq
    rblk = lax.broadcasted_iota(jnp.int32, (rows, D_FOX), 0) // nq
    cblk = lax.broadcasted_iota(jnp.int32, (rows, D_FOX), 1) // FOX_HEAD_DIM
    own_head = rblk == cblk

    @pl.when(c == 0)
    def _():
        qt = jnp.concatenate([q_ref[...]] * FOX_HEADS, axis=0)
        qbd_ref[...] = jnp.where(own_head, qt, jnp.zeros_like(qt))
        m_ref[...] = jnp.full(m_ref.shape, NEG_INF, F32)
        l_ref[...] = jnp.zeros(l_ref.shape, F32)
        acc_ref[...] = jnp.zeros(acc_ref.shape, F32)

    def expand_heads(f):
        n = f.shape[1]
        return jnp.concatenate([jnp.broadcast_to(f[h:h + 1, :], (nq, n)) for h in range(FOX_HEADS)], axis=0)

    def update(kt, vt, fk, mask):
        s = _dot(qbd_ref[...], kt) + fq_ref[...] - expand_heads(fk)
        if mask is not None:
            s = jnp.where(mask, s, NEG_INF)
        m_prev = m_ref[...]
        m_new = jnp.maximum(m_prev, jnp.max(s, axis=1, keepdims=True))
        alpha = jnp.exp(m_prev - m_new)
        p = jnp.exp(s - m_new)
        l_ref[...] = alpha * l_ref[...] + jnp.sum(p, axis=1, keepdims=True)
        acc_ref[...] = alpha * acc_ref[...] + _dot_nt(p.astype(BF16), vt)
        m_ref[...] = m_new

    off = pl.multiple_of(c * SAMPLE_KC, SAMPLE_KC)
    update(ck_ref[...].astype(BF16), cv_ref[...].astype(BF16), fk_ref[:, pl.ds(off, SAMPLE_KC)], None)

    @pl.when(c == nc - 1)
    def _():
        qpos = lax.broadcasted_iota(jnp.int32, (rows, nq), 0) % nq
        kpos = lax.broadcasted_iota(jnp.int32, (rows, nq), 1)
        update(kn_ref[...], vn_ref[...], fk_ref[:, past:past + nq], kpos <= qpos)
        o_full = jnp.where(own_head, acc_ref[...] / l_ref[...], 0.0)
        o = o_full[0:nq, :]
        for h in range(1, FOX_HEADS):
            o = o + o_full[h * nq:(h + 1) * nq, :]
        o_ref[...] = _rms(o, g_ref[...]).astype(BF16)


def _attn_sample_call(qb, ktb, vtb, cache_kt, cache_vt, fq_col, f_row, g_fox, batch, nq, past):
    rows = FOX_HEADS * nq
    nc = past // SAMPLE_KC
    per_b = lambda w: pl.BlockSpec((None, nq, w), lambda b, c: (b, 0, 0))
    new_t = pl.BlockSpec((None, D_FOX, nq), lambda b, c: (b, 0, 0))
    cache = pl.BlockSpec((None, D_FOX, SAMPLE_KC), lambda b, c: (b, 0, c))
    return pl.pallas_call(
        functools.partial(_attn_sample_kernel, nq=nq, past=past),
        grid=(batch, nc),
        in_specs=[per_b(D_FOX), new_t, new_t, cache, cache,
                  pl.BlockSpec((None, rows, 1), lambda b, c: (b, 0, 0)),
                  pl.BlockSpec((None, FOX_HEADS, f_row.shape[-1]), lambda b, c: (b, 0, 0)),
                  pl.BlockSpec((1, D_FOX), lambda b, c: (0, 0))],
        out_specs=per_b(D_FOX),
        out_shape=jax.ShapeDtypeStruct((batch, nq, D_FOX), BF16),
        scratch_shapes=[pltpu.VMEM((rows, D_FOX), BF16), pltpu.VMEM((rows, 1), F32),
                        pltpu.VMEM((rows, 1), F32), pltpu.VMEM((rows, D_FOX), F32)],
        compiler_params=_params(2),
        name="fox_attn_sample",
    )(qb.reshape(batch, nq, D_FOX), ktb, vtb, cache_kt, cache_vt, fq_col, f_row, g_fox.reshape(1, D_FOX))


def _s5_prep_kernel(lre_ref, lim_ref, ldt_ref, bre_ref, bim_ref, cre_ref, cim_ref,
                    lre_row_ref, lim_row_ref, ldt_row_ref, bbd_ref, cbd_ref, acon_ref):
    def a_bar(lre, lim, ldt):
        dt = jnp.exp(ldt)
        mag = jnp.exp(lre * dt)
        return mag * jnp.cos(lim * dt), mag * jnp.sin(lim * dt)

    lre, lim = lre_ref[...], lim_ref[...]
    ar, ai = a_bar(lre, lim, ldt_ref[...])
    den = lre * lre + lim * lim
    cr = ((ar - 1.0) * lre + ai * lim) / den
    ci = (ai * lre - (ar - 1.0) * lim) / den
    bre, bim = bre_ref[...], bim_ref[...]
    b_r = (cr * bre - ci * bim).astype(BF16)
    b_i = (cr * bim + ci * bre).astype(BF16)
    srow = lax.broadcasted_iota(jnp.int32, (S5_STATE, N_STATE), 0)
    scol = lax.broadcasted_iota(jnp.int32, (S5_STATE, N_STATE), 1)
    rep_state = jnp.where(scol % S5_STATE == srow, 1.0, 0.0).astype(BF16)
    grow = lax.broadcasted_iota(jnp.int32, (D_S5, N_STATE), 0) // S5_GROUP
    gcol = lax.broadcasted_iota(jnp.int32, (D_S5, N_STATE), 1) // S5_STATE
    own = grow == gcol
    bbd_ref[:, :N_STATE] = jnp.where(own, _dot(b_r, rep_state), 0.0).astype(BF16)
    bbd_ref[:, N_STATE:] = jnp.where(own, _dot(b_i, rep_state), 0.0).astype(BF16)

    crow = lax.broadcasted_iota(jnp.int32, (S5_GROUP, D_S5), 0)
    ccol = lax.broadcasted_iota(jnp.int32, (S5_GROUP, D_S5), 1)
    rep_chan = jnp.where(ccol % S5_GROUP == crow, 1.0, 0.0).astype(BF16)
    grow2 = lax.broadcasted_iota(jnp.int32, (N_STATE, D_S5), 0) // S5_STATE
    gcol2 = lax.broadcasted_iota(jnp.int32, (N_STATE, D_S5), 1) // S5_GROUP
    own2 = grow2 == gcol2
    cbd_ref[:N_STATE, :] = jnp.where(own2, _dot(cre_ref[...].astype(BF16), rep_chan), 0.0).astype(BF16)
    cbd_ref[N_STATE:, :] = jnp.where(own2, -_dot(cim_ref[...].astype(BF16), rep_chan), 0.0).astype(BF16)

    ar1, ai1 = a_bar(lre_row_ref[...], lim_row_ref[...], ldt_row_ref[...])
    ar2 = ar1 * ar1 - ai1 * ai1
    ai2 = 2.0 * ar1 * ai1
    upper = lax.broadcasted_iota(jnp.int32, (SUBLANES, N_STATE), 0) < SUBLANES // 2
    bc = lambda v: jnp.broadcast_to(v, (SUBLANES, N_STATE))
    acon_ref[0] = bc(ar1)
    acon_ref[1] = bc(ai1)
    acon_ref[2] = jnp.where(upper, bc(ar1), bc(ar2))
    acon_ref[3] = jnp.where(upper, bc(ai1), bc(ai2))
    acon_ref[4] = jnp.where(upper, 0.0, bc(ar1))
    acon_ref[5] = jnp.where(upper, 0.0, bc(ai1))


def _s5_prep_call(lam_re, lam_im, log_dt, b_re, b_im, c_re, c_im):
    rep_rows = lambda a: jnp.repeat(a, S5_GROUP, axis=0)
    ldt_gp = jnp.broadcast_to(log_dt[:, None], (S5_GROUPS, S5_STATE))
    to_cp = lambda b: jnp.transpose(b, (0, 2, 1)).reshape(D_S5, S5_STATE)
    to_pc = lambda c: jnp.transpose(c, (0, 2, 1)).reshape(N_STATE, S5_GROUP)
    row = lambda a: a.reshape(1, N_STATE)
    args = (rep_rows(lam_re), rep_rows(lam_im), rep_rows(ldt_gp), to_cp(b_re), to_cp(b_im),
            to_pc(c_re), to_pc(c_im), row(lam_re), row(lam_im), row(ldt_gp))
    sd = jax.ShapeDtypeStruct
    return pl.pallas_call(
        _s5_prep_kernel,
        out_shape=[sd((D_S5, 2 * N_STATE), BF16), sd((2 * N_STATE, D_S5), BF16),
                   sd((6, SUBLANES, N_STATE), F32)],
        compiler_params=pltpu.CompilerParams(vmem_limit_bytes=VMEM_LIMIT),
        name="s5_prep",
    )(*args)


SCAN_COLS = 512


def _s5_kernel(u_ref, h0_ref, bbd_ref, cbd_ref, acon_ref, d_ref, wglu_ref, bglu_ref, gs_ref,
               y_ref, st_ref, ui_ref, s_ref, yi_ref, h_ref, *, nb):
    i = pl.program_id(0)
    tt = u_ref.shape[1]
    rows = nb * tt
    per = SUBLANES // nb
    half_in = D_S5 // 2
    half_st = N_STATE // 2

    @pl.when(i == 0)
    def _():
        h_ref[...] = h0_ref[...]

    strips = D_S5 // LANES
    for b in range(nb):
        for c in range(strips):
            yi_ref[c, b * tt:(b + 1) * tt, :] = u_ref[b, :, c * LANES:(c + 1) * LANES]
    for k in range(rows // SUBLANES):
        for c in range(strips):
            parts = [yi_ref[c, pl.ds(k * per + s, nb, stride=tt), :] for s in range(per)]
            ui_ref[k * SUBLANES:(k + 1) * SUBLANES, c * LANES:(c + 1) * LANES] = (
                parts[0] if per == 1 else jnp.concatenate(parts, axis=0))

    ub = ui_ref[...].astype(BF16)
    for h in range(2):
        x = ub[:, h * half_in:(h + 1) * half_in]
        for part in range(2):
            cols = slice(part * N_STATE + h * half_st, part * N_STATE + (h + 1) * half_st)
            s_ref[:, cols] = _dot(x, bbd_ref[h * half_in:(h + 1) * half_in, cols])

    upper = lax.broadcasted_iota(jnp.int32, (SUBLANES, SCAN_COLS), 0) < SUBLANES // 2
    for c in range(N_STATE // SCAN_COLS):
        re = slice(c * SCAN_COLS, (c + 1) * SCAN_COLS)
        im = slice(N_STATE + c * SCAN_COLS, N_STATE + (c + 1) * SCAN_COLS)
        if nb == SUBLANES:
            ar, ai = acon_ref[0, :, re], acon_ref[1, :, re]

            def step(k, carry):
                hr, hi = carry
                r0 = pl.multiple_of(k * SUBLANES, SUBLANES)
                nr = ar * hr - ai * hi + s_ref[pl.ds(r0, SUBLANES), re]
                ni = ar * hi + ai * hr + s_ref[pl.ds(r0, SUBLANES), im]
                s_ref[pl.ds(r0, SUBLANES), re] = nr
                s_ref[pl.ds(r0, SUBLANES), im] = ni
                return nr, ni
        else:
            a2r, a2i = acon_ref[2, :, re], acon_ref[3, :, re]
            alr, ali = acon_ref[4, :, re], acon_ref[5, :, re]

            def step(k, carry):
                hr, hi = carry
                r0 = pl.multiple_of(k * SUBLANES, SUBLANES)
                xr = s_ref[pl.ds(r0, SUBLANES), re]
                xi = s_ref[pl.ds(r0, SUBLANES), im]
                sr = pltpu.roll(xr, SUBLANES // 2, 0)
                si = pltpu.roll(xi, SUBLANES // 2, 0)
                nr = (a2r * hr - a2i * hi) + (alr * sr - ali * si) + xr
                ni = (a2r * hi + a2i * hr) + (alr * si + ali * sr) + xi
                s_ref[pl.ds(r0, SUBLANES), re] = nr
                s_ref[pl.ds(r0, SUBLANES), im] = ni
                return (jnp.where(upper, pltpu.roll(nr, SUBLANES // 2, 0), nr),
                        jnp.where(upper, pltpu.roll(ni, SUBLANES // 2, 0), ni))

        hr, hi = lax.fori_loop(0, rows // SUBLANES, step, (h_ref[0, :, re], h_ref[1, :, re]), unroll=2)
        h_ref[0, :, re] = hr
        h_ref[1, :, re] = hi

    @pl.when(i == pl.num_programs(0) - 1)
    def _():
        st_ref[...] = h_ref[...]

    halves = []
    for h in range(2):
        oc = slice(h * half_in, (h + 1) * half_in)
        sr = slice(h * half_st, (h + 1) * half_st)
        si = slice(N_STATE + h * half_st, N_STATE + (h + 1) * half_st)
        halves.append(_dot(s_ref[:, sr].astype(BF16), cbd_ref[sr, oc])
                      + _dot(s_ref[:, si].astype(BF16), cbd_ref[si, oc]))
    y = jnp.concatenate(halves, axis=1) + d_ref[...] * ui_ref[...]
    y = 0.5 * y * (1.0 + jnp.tanh(math.sqrt(2.0 / math.pi) * (y + 0.044715 * (y * y * y))))
    y = y * _sigmoid(_dot(y.astype(BF16), wglu_ref[...]) + bglu_ref[...])
    y = _rms(y, gs_ref[...])
    for c in range(strips):
        yi_ref[c] = y[:, c * LANES:(c + 1) * LANES]
    for b in range(nb):
        y_ref[b] = jnp.concatenate([yi_ref[c, pl.ds(b, tt, stride=nb), :] for c in range(strips)],
                                   axis=1).astype(BF16)


def _s5_call(u, h0, bbd, cbd, acon, s5_d, wglu, b_glu, g_s5, tt, name):
    nb, seq, _ = u.shape
    rows = tt * nb
    blk = pl.BlockSpec((nb, tt, D_S5), lambda i: (0, i, 0))
    vec = lambda a: a.reshape(1, D_S5)
    return pl.pallas_call(
        functools.partial(_s5_kernel, nb=nb),
        grid=(seq // tt,),
        in_specs=[blk, _resident(h0.shape), _resident(bbd.shape), _resident(cbd.shape),
                  _resident(acon.shape), _resident((1, D_S5)), _resident(wglu.shape),
                  _resident((1, D_S5)), _resident((1, D_S5))],
        out_specs=[blk, pl.BlockSpec(h0.shape, lambda i: (0, 0, 0))],
        out_shape=[jax.ShapeDtypeStruct((nb, seq, D_S5), BF16), jax.ShapeDtypeStruct(h0.shape, F32)],
        scratch_shapes=[pltpu.VMEM((rows, D_S5), F32), pltpu.VMEM((rows, 2 * N_STATE), F32),
                        pltpu.VMEM((D_S5 // LANES, rows, LANES), F32), pltpu.VMEM(h0.shape, F32)],
        compiler_params=_params(),
        name=name,
    )(u, h0, bbd, cbd, acon, vec(s5_d), wglu, vec(b_glu), vec(g_s5))


FFN_TM = 512
S5_TT = 128


def kernel(x_prompt, x_sample, c_prompt, c_sample, cache_fox_k, cache_fox_v, cache_fox_logf, state_s5_re, state_s5_im, w_ada, b_ada, g_ffn1, w_up1, w_down1, g_mix, w_in, b_fgate, lam_re, lam_im, log_dt, s5_b_re, s5_b_im, s5_c_re, s5_c_im, s5_d, w_glu, b_glu, g_s5_out, g_fox_out, w_out, g_ffn2, w_up2, w_down2, g_final):
    assert w_ada.shape[0] == 1, "single-layer trunk"
    bp, seq, _ = x_prompt.shape
    bs, nq, _ = x_sample.shape
    past = cache_fox_k.shape[2]
    n_p, n_s = bp * seq, bs * nq

    wup1, wdn1 = w_up1[0].astype(BF16), w_down1[0].astype(BF16)
    wup2, wdn2 = w_up2[0].astype(BF16), w_down2[0].astype(BF16)
    w_inb = w_in[0].astype(BF16)
    w_nat_s = w_inb[:, :D_S5 + 3 * D_FOX]
    w_nat_p = jnp.concatenate([w_inb[:, :D_S5], w_inb[:, D_S5 + D_FOX:D_S5 + 2 * D_FOX]], axis=1)
    w_t = jnp.pad(w_inb[:, D_S5:].T, ((0, F_ROWS - FOX_HEADS), (0, 0)))
    b_f = jnp.pad(b_fgate[0], (0, F_ROWS - FOX_HEADS)).reshape(F_ROWS, 1)
    wo = w_out[0].astype(BF16)
    wglu = w_glu[0].astype(BF16)

    pad_rows = -(bp + bs) % SUBLANES
    c_all = jnp.concatenate([c_prompt, c_sample, jnp.zeros((pad_rows, D_MODEL), F32)], axis=0)
    mod = _ada_call(c_all, w_ada[0], b_ada[0])
    mod_p = mod[:bp].reshape(bp, 1, N_MOD * D_MODEL)
    mod_s = jnp.repeat(mod[bp:bp + bs], nq, axis=0)

    bbd, cbd, acon = _s5_prep_call(lam_re[0], lam_im[0], log_dt[0], s5_b_re[0], s5_b_im[0],
                                   s5_c_re[0], s5_c_im[0])

    tm = FFN_TM
    tpb = seq // tm
    xp = x_prompt.reshape(n_p, D_MODEL)
    x1 = _ffn_call(xp, mod_p, 0, False, tm, tpb, g_ffn1[0], wup1, wdn1, name="ffn1_prompt")
    u_p, qtb, kb, vtb, kt_p, vt_p, lft_p = _inproj_call(
        x1, mod_p, False, tm, tpb, g_mix[0], w_nat_p, w_t, b_f, True, "inproj_prompt")
    f_row = _cumsum_call(lft_p, "cumsum_prompt")
    f_col = jnp.transpose(f_row, (0, 2, 1))
    o_p = _attn_call(qtb, kb, vtb, f_col, f_row, g_fox_out[0], bp, seq)
    h0_p = jnp.zeros((2, SUBLANES, N_STATE), F32)
    ys_p, st_p = _s5_call(u_p.reshape(bp, seq, D_S5), h0_p, bbd, cbd, acon, s5_d[0], wglu, b_glu[0],
                          g_s5_out[0], S5_TT, "s5_prompt")
    y_p = _ffn_call(x1, mod_p, 6, False, tm, tpb, g_ffn2[0], wup2, wdn2,
                    premix=(ys_p.reshape(n_p, D_S5), pl.BlockSpec((tm, D_S5), lambda i: (i, 0)),
                            o_p.reshape(n_p, D_FOX), wo, 5),
                    final_g=g_final, name="ffn2_prompt")

    xs = x_sample.reshape(n_s, D_MODEL)
    xs1 = _ffn_call(xs, mod_s, 0, True, n_s, 1, g_ffn1[0], wup1, wdn1, name="ffn1_sample")
    u_s, qb_s, ktb_s, vtb_s, k_s, v_s, lft_s = _inproj_call(
        xs1, mod_s, True, n_s, 1, g_mix[0], w_nat_s, w_t, b_f, False, "inproj_sample")
    per_batch_t = lambda a: jnp.transpose(a.reshape(a.shape[0], bs, nq), (1, 0, 2))
    lft_s = per_batch_t(lft_s)
    f_pad = -(past + nq) % CUM_CHUNK
    lft_all = jnp.concatenate([jnp.transpose(cache_fox_logf[0], (0, 2, 1)), lft_s,
                               jnp.zeros((bs, FOX_HEADS, f_pad), F32)], axis=2)
    f_row_s = _cumsum_call(lft_all, "cumsum_sample")
    fq_col = f_row_s[:, :, past:past + nq].reshape(bs, FOX_HEADS * nq, 1)
    feature_major = lambda c: jnp.transpose(c, (0, 2, 3, 1)).reshape(bs, D_FOX, past)
    o_s = _attn_sample_call(qb_s, per_batch_t(ktb_s), per_batch_t(vtb_s), feature_major(cache_fox_k[0]),
                            feature_major(cache_fox_v[0]), fq_col, f_row_s, g_fox_out[0], bs, nq, past)
    h0_s = jnp.stack([state_s5_re[0].reshape(bs, N_STATE), state_s5_im[0].reshape(bs, N_STATE)])
    ys_s, st_s = _s5_call(u_s.reshape(bs, nq, D_S5), h0_s, bbd, cbd, acon, s5_d[0], wglu, b_glu[0],
                          g_s5_out[0], nq, "s5_sample")
    y_s = _ffn_call(xs1, mod_s, 6, True, n_s, 1, g_ffn2[0], wup2, wdn2,
                    premix=(ys_s.reshape(n_s, D_S5), pl.BlockSpec((n_s, D_S5), lambda i: (i, 0)),
                            o_s.reshape(n_s, D_FOX), wo, 5),
                    final_g=g_final, name="ffn2_sample")

    heads = lambda a, b, t: a.reshape(1, b, t, FOX_HEADS, FOX_HEAD_DIM)
    heads_t = lambda a, b, t: jnp.transpose(a.reshape(1, b, FOX_HEADS, FOX_HEAD_DIM, t), (0, 1, 4, 2, 3))
    gates_t = lambda a: jnp.transpose(a, (0, 2, 1))[None]
    state = lambda a, b: a[:b].reshape(1, b, S5_GROUPS, S5_STATE)
    return (y_p.reshape(bp, seq, D_MODEL), y_s.reshape(bs, nq, D_MODEL),
            heads_t(kt_p, bp, seq), heads_t(vt_p, bp, seq), gates_t(lft_p),
            state(st_p[0], bp), state(st_p[1], bp),
            heads(k_s, bs, nq), heads(v_s, bs, nq), gates_t(lft_s),
            state(st_s[0], bs), state(st_s[1], bs))
```
